```python
import jax, jax.numpy as jnp
from jax import lax
import numpy as np

D_MODEL = 1024
BATCH = 8
SEQ = 4096
DEPTH = 1
DEC_BATCH = 128
DEC_SEQ = 1
PAST_LEN = 8192
PAGE_SIZE = 128

HEAD_DIM = 64
H_FOX = 8
H_SB = 4
H_MEM = 4
FOX_W = H_FOX * HEAD_DIM
SB_W = H_SB * HEAD_DIM
MEM_W = H_MEM * HEAD_DIM
N_MEM = 256
N_BRANCH = 3
IN_SPLITS = (FOX_W, FOX_W, FOX_W, H_FOX, SB_W, SB_W, SB_W, MEM_W, N_BRANCH * D_MODEL)
IN_W = FOX_W * 3 + H_FOX + SB_W * 3 + MEM_W + N_BRANCH * D_MODEL
Q_BLOCK = 128
ATTN_SCALE = HEAD_DIM ** -0.5
FORGET_BIAS_INIT = 2.0
N_GROUPS = 4
EXPERTS_PER_GROUP = 4
N_EXPERTS = N_GROUPS * EXPERTS_PER_GROUP
TOP_K_IN_GROUP = 2
D_EXPERT = 512
MOE_BLOCK = 128
POOL_NUM = 5
POOL_DEN = 4
EPS = 1e-6

kernel_name = 'fox_stickbreak_hmoe_decoder_step'


def rms_norm(x, g):
    xf = x.astype(jnp.float32)
    y = xf * lax.rsqrt(jnp.mean(xf * xf, axis=-1, keepdims=True) + EPS)
    return (y * g.astype(jnp.float32)).astype(x.dtype)


def flat_heads(o):
    return o.reshape(*o.shape[:-2], o.shape[-2] * o.shape[-1])


def gather_pages(pool, layer, page_table):
    g = pool[layer, page_table]
    return g.reshape(g.shape[0], g.shape[1] * g.shape[2], *g.shape[3:])


def project_in(xn, w_in, b_forget, g_fox_q, g_fox_k, g_mem_q):
    lead = xn.shape[:-1]
    offs = np.cumsum(IN_SPLITS)[:-1].tolist()
    q_f, k_f, v_f, f_pre, q_s, k_s, v_s, q_m, g_pre = jnp.split(xn @ w_in, offs, axis=-1)

    def heads(t, h):
        return t.reshape(*lead, h, HEAD_DIM)

    q_f = rms_norm(heads(q_f, H_FOX), g_fox_q)
    k_f = rms_norm(heads(k_f, H_FOX), g_fox_k)
    logf = jax.nn.log_sigmoid((f_pre + b_forget).astype(jnp.float32)).astype(xn.dtype)
    q_m = rms_norm(heads(q_m, H_MEM), g_mem_q)
    gate = jax.nn.sigmoid(g_pre.reshape(*lead, N_BRANCH, D_MODEL))
    return (q_f, k_f, heads(v_f, H_FOX), logf, heads(q_s, H_SB), heads(k_s, H_SB),
            heads(v_s, H_SB), q_m, gate)


def fox_attend(q, c_q, k, v, c_k, q_pos, k_pos):
    s = jnp.einsum('bqhd,bkhd->bhqk', q, k).astype(jnp.float32) * ATTN_SCALE
    bias = jnp.transpose(c_q, (0, 2, 1))[..., :, None] - jnp.transpose(c_k, (0, 2, 1))[..., None, :]
    mask = k_pos[None, :] <= q_pos[:, None]
    p = jax.nn.softmax(jnp.where(mask, s + bias, -jnp.inf), axis=-1)
    return jnp.einsum('bhqk,bkhd->bqhd', p.astype(v.dtype), v)


def sb_attend(q, k, v, q_pos, k_pos):
    z = jnp.einsum('bqhd,bkhd->bhqk', q, k).astype(jnp.float32) * ATTN_SCALE
    mask = k_pos[None, :] < q_pos[:, None]
    log_1mb = jnp.where(mask, jax.nn.log_sigmoid(-z), 0.0)
    after = lax.cumsum(log_1mb, axis=3, reverse=True) - log_1mb
    w = jnp.where(mask, jnp.exp(jax.nn.log_sigmoid(z) + after), 0.0)
    return jnp.einsum('bhqk,bkhd->bqhd', w.astype(v.dtype), v)


def fox_prompt(q, k, v, logf):
    b, s = q.shape[:2]
    c = jnp.cumsum(logf.astype(jnp.float32), axis=1)
    k_pos = jnp.arange(s)

    def block(i):
        st = i * Q_BLOCK
        qb = lax.dynamic_slice_in_dim(q, st, Q_BLOCK, axis=1)
        cb = lax.dynamic_slice_in_dim(c, st, Q_BLOCK, axis=1)
        return fox_attend(qb, cb, k, v, c, st + jnp.arange(Q_BLOCK), k_pos)

    o = lax.map(block, jnp.arange(s // Q_BLOCK))
    return flat_heads(jnp.moveaxis(o, 0, 1).reshape(b, s, *q.shape[2:]))


def sb_prompt(q, k, v):
    b, s = q.shape[:2]
    k_pos = jnp.arange(s)

    def block(i):
        st = i * Q_BLOCK
        qb = lax.dynamic_slice_in_dim(q, st, Q_BLOCK, axis=1)
        return sb_attend(qb, k, v, st + jnp.arange(Q_BLOCK), k_pos)

    o = lax.map(block, jnp.arange(s // Q_BLOCK))
    return flat_heads(jnp.moveaxis(o, 0, 1).reshape(b, s, *q.shape[2:]))


def mem_kv(mem, norm_mem, w_mem_k, w_mem_v, g_mem_k):
    mn = rms_norm(mem, norm_mem)
    shp = (*mem.shape[:2], H_MEM, HEAD_DIM)
    return rms_norm((mn @ w_mem_k).reshape(shp), g_mem_k), (mn @ w_mem_v).reshape(shp)


def mem_attend(q, mk, mv):
    s = jnp.einsum('bqhd,bmhd->bhqm', q, mk).astype(jnp.float32) * ATTN_SCALE
    p = jax.nn.softmax(s, axis=-1)
    return flat_heads(jnp.einsum('bhqm,bmhd->bqhd', p.astype(mv.dtype), mv))


def routed_experts(xf, expert_idx, gates, w1, w3, w2):
    t = xf.shape[0]
    tk = t * TOP_K_IN_GROUP
    flat_e = expert_idx.reshape(-1)
    order = jnp.argsort(flat_e)
    e_sorted = flat_e[order]
    tok_sorted = order // TOP_K_IN_GROUP
    counts = jnp.bincount(flat_e, length=N_EXPERTS)
    padded = (counts + MOE_BLOCK - 1) // MOE_BLOCK * MOE_BLOCK
    pad_end = jnp.cumsum(padded)
    pad_start = pad_end - padded
    start = jnp.cumsum(counts) - counts
    dest = pad_start[e_sorted] + jnp.arange(tk) - start[e_sorted]
    n_blocks = (tk + N_EXPERTS * (MOE_BLOCK - 1) + MOE_BLOCK - 1) // MOE_BLOCK
    rows = jnp.zeros((n_blocks * MOE_BLOCK, D_MODEL), xf.dtype).at[dest].set(xf[tok_sorted])
    block_e = jnp.minimum(jnp.searchsorted(pad_end, jnp.arange(n_blocks) * MOE_BLOCK, side='right'),
                          N_EXPERTS - 1)

    def run(args):
        xb, e = args
        return (jax.nn.silu(xb @ w1[e]) * (xb @ w3[e])) @ w2[e]

    out = lax.map(run, (rows.reshape(n_blocks, MOE_BLOCK, D_MODEL), block_e)).reshape(-1, D_MODEL)
    contrib = out[dest] * gates.reshape(-1)[order][:, None].astype(out.dtype)
    return jax.ops.segment_sum(contrib, tok_sorted, num_segments=t)


def hier_moe(hn, w_grp, b_grp, w_exp, b_exp, w1, w3, w2):
    lead = hn.shape[:-1]
    xf = hn.reshape(-1, D_MODEL)
    grp_logits = (xf @ w_grp + b_grp).astype(jnp.float32)
    grp_prob = jax.nn.softmax(grp_logits, axis=-1)
    g_idx = jnp.argmax(grp_logits, axis=-1)
    g_gate = jnp.take_along_axis(grp_prob, g_idx[:, None], axis=-1)
    exp_logits = (xf @ w_exp + b_exp).astype(jnp.float32).reshape(-1, N_GROUPS, EXPERTS_PER_GROUP)
    in_logits = jnp.take_along_axis(exp_logits, g_idx[:, None, None], axis=1)[:, 0]
    top_p, top_i = lax.top_k(jax.nn.softmax(in_logits, axis=-1), TOP_K_IN_GROUP)
    gates = g_gate * top_p / jnp.sum(top_p, axis=-1, keepdims=True)
    expert_idx = g_idx[:, None] * EXPERTS_PER_GROUP + top_i
    return routed_experts(xf, expert_idx, gates, w1, w3, w2).reshape(*lead, D_MODEL)


def merge_and_ffn(x, o_fox, o_sb, o_mem, gate, w_up_fox, w_up_sb, w_up_mem, w_o, norm_ffn,
                  w_grp, b_grp, w_exp, b_exp, w1, w3, w2):
    m = (gate[..., 0, :] * (o_fox @ w_up_fox) + gate[..., 1, :] * (o_sb @ w_up_sb)
         + gate[..., 2, :] * (o_mem @ w_up_mem))
    h = x + m @ w_o
    return h + hier_moe(rms_norm(h, norm_ffn), w_grp, b_grp, w_exp, b_exp, w1, w3, w2)


def setup_inputs(seed: int = 0) -> dict:
    key = jax.random.key(seed)
    keys = iter(jax.random.split(key, 64))
    n_pages = PAST_LEN // PAGE_SIZE
    n_used = DEC_BATCH * n_pages
    n_pool = (n_used * POOL_NUM) // POOL_DEN
    L = DEPTH

    def normal(shape, scale=1.0):
        return scale * jax.random.normal(next(keys), shape, jnp.float32)

    def gain(shape):
        return 1.0 + 0.05 * jax.random.normal(next(keys), shape, jnp.float32)

    page_table = jax.random.permutation(next(keys), n_pool)[:n_used].reshape(DEC_BATCH, n_pages).astype(jnp.int32)
    return {
        'x_prompt': normal((BATCH, SEQ, D_MODEL)),
        'x_sample': normal((DEC_BATCH, DEC_SEQ, D_MODEL)),
        'mem_prompt': normal((BATCH, N_MEM, D_MODEL)),
        'cache_fox_k': normal((L, n_pool, PAGE_SIZE, H_FOX, HEAD_DIM)),
        'cache_fox_v': normal((L, n_pool, PAGE_SIZE, H_FOX, HEAD_DIM)),
        'cache_fox_logf': jax.nn.log_sigmoid(FORGET_BIAS_INIT + normal((L, n_pool, PAGE_SIZE, H_FOX))),
        'cache_sb_k': normal((L, n_pool, PAGE_SIZE, H_SB, HEAD_DIM)),
        'cache_sb_v': normal((L, n_pool, PAGE_SIZE, H_SB, HEAD_DIM)),
        'cache_mem_k': normal((L, DEC_BATCH, N_MEM, H_MEM, HEAD_DIM)),
        'cache_mem_v': normal((L, DEC_BATCH, N_MEM, H_MEM, HEAD_DIM)),
        'page_table': page_table,
        'norm_attn': gain((L, D_MODEL)),
        'w_in': normal((L, D_MODEL, IN_W), D_MODEL ** -0.5),
        'b_forget': FORGET_BIAS_INIT + normal((L, H_FOX), 0.1),
        'g_fox_q': gain((L, HEAD_DIM)),
        'g_fox_k': gain((L, HEAD_DIM)),
        'g_mem_q': gain((L, HEAD_DIM)),
        'g_mem_k': gain((L, HEAD_DIM)),
        'norm_mem': gain((L, D_MODEL)),
        'w_mem_k': normal((L, D_MODEL, MEM_W), D_MODEL ** -0.5),
        'w_mem_v': normal((L, D_MODEL, MEM_W), D_MODEL ** -0.5),
        'w_up_fox': normal((L, FOX_W, D_MODEL), FOX_W ** -0.5),
        'w_up_sb': normal((L, SB_W, D_MODEL), SB_W ** -0.5),
        'w_up_mem': normal((L, MEM_W, D_MODEL), MEM_W ** -0.5),
        'w_o': normal((L, D_MODEL, D_MODEL), D_MODEL ** -0.5),
        'norm_ffn': gain((L, D_MODEL)),
        'w_grp': normal((L, D_MODEL, N_GROUPS), D_MODEL ** -0.5),
        'b_grp': normal((L, N_GROUPS), 0.01),
        'w_exp': normal((L, D_MODEL, N_EXPERTS), D_MODEL ** -0.5),
        'b_exp': normal((L, N_EXPERTS), 0.01),
        'w1': normal((L, N_EXPERTS, D_MODEL, D_EXPERT), D_MODEL ** -0.5),
        'w3': normal((L, N_EXPERTS, D_MODEL, D_EXPERT), D_MODEL ** -0.5),
        'w2': normal((L, N_EXPERTS, D_EXPERT, D_MODEL), D_EXPERT ** -0.5),
    }


def reference(x_prompt, x_sample, mem_prompt, cache_fox_k, cache_fox_v, cache_fox_logf,
              cache_sb_k, cache_sb_v, cache_mem_k, cache_mem_v, page_table,
              norm_attn, w_in, b_forget, g_fox_q, g_fox_k, g_mem_q, g_mem_k, norm_mem,
              w_mem_k, w_mem_v, w_up_fox, w_up_sb, w_up_mem, w_o, norm_ffn,
              w_grp, b_grp, w_exp, b_exp, w1, w3, w2):
    xp, xs = x_prompt, x_sample
    past_len = page_table.shape[1] * cache_fox_k.shape[2]
    n_new = xs.shape[1]
    q_pos_s = past_len + jnp.arange(n_new)
    k_pos_s = jnp.arange(past_len + n_new)
    p_fk, p_fv, p_fl, p_sk, p_sv, p_mk, p_mv = [], [], [], [], [], [], []
    s_fk, s_fv, s_fl, s_sk, s_sv = [], [], [], [], []
    for l in range(DEPTH):
        ffn_w = (w_up_fox[l], w_up_sb[l], w_up_mem[l], w_o[l], norm_ffn[l],
                 w_grp[l], b_grp[l], w_exp[l], b_exp[l], w1[l], w3[l], w2[l])
        qf, kf, vf, lf, qs, ks, vs, qm, gate = project_in(
            rms_norm(xp, norm_attn[l]), w_in[l], b_forget[l], g_fox_q[l], g_fox_k[l], g_mem_q[l])
        mk, mv = mem_kv(mem_prompt, norm_mem[l], w_mem_k[l], w_mem_v[l], g_mem_k[l])
        o_f = fox_prompt(qf, kf, vf, lf)
        o_s = sb_prompt(qs, ks, vs)
        o_m = mem_attend(qm, mk, mv)
        xp = merge_and_ffn(xp, o_f, o_s, o_m, gate, *ffn_w)
        p_fk.append(kf); p_fv.append(vf); p_fl.append(lf)
        p_sk.append(ks); p_sv.append(vs); p_mk.append(mk); p_mv.append(mv)
        qf, kf, vf, lf, qs, ks, vs, qm, gate = project_in(
            rms_norm(xs, norm_attn[l]), w_in[l], b_forget[l], g_fox_q[l], g_fox_k[l], g_mem_q[l])
        kf_all = jnp.concatenate([gather_pages(cache_fox_k, l, page_table), kf], axis=1)
        vf_all = jnp.concatenate([gather_pages(cache_fox_v, l, page_table), vf], axis=1)
        lf_all = jnp.concatenate([gather_pages(cache_fox_logf, l, page_table), lf], axis=1)
        c = jnp.cumsum(lf_all.astype(jnp.float32), axis=1)
        o_f = flat_heads(fox_attend(qf, c[:, past_len:], kf_all, vf_all, c, q_pos_s, k_pos_s))
        ks_all = jnp.concatenate([gather_pages(cache_sb_k, l, page_table), ks], axis=1)
        vs_all = jnp.concatenate([gather_pages(cache_sb_v, l, page_table), vs], axis=1)
        o_s = flat_heads(sb_attend(qs, ks_all, vs_all, q_pos_s, k_pos_s))
        o_m = mem_attend(qm, cache_mem_k[l], cache_mem_v[l])
        xs = merge_and_ffn(xs, o_f, o_s, o_m, gate, *ffn_w)
        s_fk.append(kf); s_fv.append(vf); s_fl.append(lf); s_sk.append(ks); s_sv.append(vs)
    return (xp, xs,
            jnp.stack(p_fk), jnp.stack(p_fv), jnp.stack(p_fl), jnp.stack(p_sk), jnp.stack(p_sv),
            jnp.stack(p_mk), jnp.stack(p_mv),
            jnp.stack(s_fk), jnp.stack(s_fv), jnp.stack(s_fl), jnp.stack(s_sk), jnp.stack(s_sv))
```

```python
import functools

import numpy as np
import jax
import jax.numpy as jnp
from jax import lax
from jax.experimental import pallas as pl
from jax.experimental.pallas import tpu as pltpu

BF = jnp.bfloat16
F32 = jnp.float32

D_MODEL = 1024
HEAD_DIM = 64
H_FOX = 8
H_SB = 4
H_MEM = 4
FOX_W = H_FOX * HEAD_DIM
SB_W = H_SB * HEAD_DIM
MEM_W = H_MEM * HEAD_DIM
N_GROUPS = 4
EXPERTS_PER_GROUP = 4
N_EXPERTS = N_GROUPS * EXPERTS_PER_GROUP
D_EXPERT = 512
EPS = 1e-6
ATTN_SCALE = HEAD_DIM ** -0.5
LANES = 128
A_W = 3 * FOX_W + 3 * SB_W + MEM_W
VMEM_LIMIT = 56 * 1024 * 1024
PROMPT_TM = 256
ATTN_T = 256
MOE_TT = 1024
MOE_R = 256
DECODE_PAGES_PER_STEP = 4


def _dot(a, b):
    return jnp.dot(a, b, preferred_element_type=F32)


def _dot_nt(a, b):
    return lax.dot_general(a, b, (((1,), (1,)), ((), ())), preferred_element_type=F32)


def _split3(a):
    a1 = a.astype(BF)
    r = a - a1.astype(F32)
    a2 = r.astype(BF)
    a3 = (r - a2.astype(F32)).astype(BF)
    return a1, a2, a3


def _exact_left(m, b):
    b1, b2, b3 = _split3(b)
    return _dot(m, b1) + _dot(m, b2) + _dot(m, b3)


def _exact_right(a, m):
    a1, a2, a3 = _split3(a)
    return _dot(a1, m) + _dot(a2, m) + _dot(a3, m)


def _log_sigmoid(x):
    return jnp.minimum(x, 0.0) - jnp.log1p(jnp.exp(-jnp.abs(x)))


def _rms(x, g):
    return x * lax.rsqrt(jnp.mean(x * x, axis=-1, keepdims=True) + EPS) * g


def _head_norm(t, head_mean, g):
    ms = _dot((t * t).astype(BF), head_mean)
    return t * lax.rsqrt(ms + EPS) * g


def _const_spec(shape):
    return pl.BlockSpec(shape, lambda *_: (0,) * len(shape))


def _head_mean_matrix(width):
    i = np.arange(width)
    return jnp.asarray((i[:, None] // HEAD_DIM == i[None, :] // HEAD_DIM) / HEAD_DIM, BF)


def _inproj_body(x_ref, na_ref, wa_ref, wf_ref, bfg_ref, gfq_ref, gfk_ref, gmq_ref, hm512_ref, hm256_ref,
                 tri_ref, exp_ref,
                 qf_ref, kf_ref, kfb_ref, vf_ref, vfb_ref, lf_ref, cexp_ref, qs_ref, ks_ref, ksb_ref,
                 vs_ref, vsb_ref, qm_ref, carry_ref):
    tm = x_ref.shape[1]

    @pl.when(pl.program_id(1) == 0)
    def _():
        carry_ref[...] = jnp.zeros_like(carry_ref)

    xn = _rms(x_ref[0], na_ref[...]).astype(BF)
    y = _dot(xn, wa_ref[...])
    f = _dot(xn, wf_ref[...])

    qf = _head_norm(y[:, 0:FOX_W], hm512_ref[...], gfq_ref[...])
    qf_ref[0] = (qf * ATTN_SCALE).astype(BF)
    kf = _head_norm(y[:, FOX_W:2 * FOX_W], hm512_ref[...], gfk_ref[...])
    kf_ref[0] = kf
    kfb_ref[0] = kf.astype(BF)
    vf = y[:, 2 * FOX_W:3 * FOX_W]
    vf_ref[0] = vf
    vfb_ref[0] = vf.astype(BF)
    o = 3 * FOX_W
    qs_ref[0] = (y[:, o:o + SB_W] * ATTN_SCALE).astype(BF)
    ks = y[:, o + SB_W:o + 2 * SB_W]
    ks_ref[0] = ks
    ksb_ref[0] = ks.astype(BF)
    vs = y[:, o + 2 * SB_W:o + 3 * SB_W]
    vs_ref[0] = vs
    vsb_ref[0] = vs.astype(BF)
    qm = _head_norm(y[:, o + 3 * SB_W:o + 3 * SB_W + MEM_W], hm256_ref[...], gmq_ref[...])
    qm_ref[0] = (qm * ATTN_SCALE).astype(BF)

    lane = lax.broadcasted_iota(jnp.int32, (tm, LANES), 1)
    lf = jnp.where(lane < H_FOX, _log_sigmoid(f + bfg_ref[...]), 0.0)
    lf_ref[0] = lf
    c = carry_ref[...] + _exact_left(tri_ref[...], lf)
    carry_ref[...] = c[tm - 1:tm, :]
    cexp_ref[0] = _exact_right(c, exp_ref[...])


def _inproj(x, norm_attn, wa, wf, b_forget, g_fox_q, g_fox_k, g_mem_q, tm):
    b, s, _ = x.shape
    nt = s // tm
    tile = lambda w: pl.BlockSpec((1, tm, w), lambda i, j: (i, j, 0))
    rep = lambda g, n: jnp.tile(g.reshape(1, HEAD_DIM), (1, n))
    bfg = jnp.zeros((1, LANES), F32).at[0, :H_FOX].set(b_forget)
    tri = jnp.asarray(np.tril(np.ones((tm, tm))), BF)
    i = np.arange(FOX_W)
    expand = jnp.asarray(np.arange(LANES)[:, None] == i[None, :] // HEAD_DIM, BF)
    shapes = [(FOX_W, BF), (FOX_W, F32), (FOX_W, BF), (FOX_W, F32), (FOX_W, BF), (LANES, F32), (FOX_W, F32),
              (SB_W, BF), (SB_W, F32), (SB_W, BF), (SB_W, F32), (SB_W, BF), (MEM_W, BF)]
    return pl.pallas_call(
        _inproj_body,
        grid=(b, nt),
        in_specs=[tile(D_MODEL), _const_spec((1, D_MODEL)), _const_spec((D_MODEL, A_W)),
                  _const_spec((D_MODEL, LANES)), _const_spec((1, LANES)), _const_spec((1, FOX_W)),
                  _const_spec((1, FOX_W)), _const_spec((1, MEM_W)), _const_spec((FOX_W, FOX_W)),
                  _const_spec((MEM_W, MEM_W)), _const_spec((tm, tm)), _const_spec((LANES, FOX_W))],
        out_specs=[tile(w) for w, _ in shapes],
        out_shape=[jax.ShapeDtypeStruct((b, s, w), d) for w, d in shapes],
        scratch_shapes=[pltpu.VMEM((1, LANES), F32)],
        compiler_params=pltpu.CompilerParams(dimension_semantics=("arbitrary", "arbitrary"),
                                             vmem_limit_bytes=VMEM_LIMIT),
        name="inproj",
    )(x, norm_attn.reshape(1, D_MODEL), wa, wf, bfg, rep(g_fox_q, H_FOX), rep(g_fox_k, H_FOX),
      rep(g_mem_q, H_MEM), _head_mean_matrix(FOX_W), _head_mean_matrix(MEM_W), tri, expand)


def _fox_body(q_ref, k_ref, v_ref, cq_ref, ck_ref, o_ref, *, t):
    qi = pl.program_id(2)
    q = q_ref[0].astype(F32)
    cq = cq_ref[0]
    lane = lax.broadcasted_iota(jnp.int32, (t, LANES), 1)
    row = lax.broadcasted_iota(jnp.int32, (t, t), 0)
    col = lax.broadcasted_iota(jnp.int32, (t, t), 1)
    qh = [jnp.where(lane // HEAD_DIM == j, q, 0.0).astype(BF) for j in range(2)]
    cqh = [cq[:, j * HEAD_DIM:j * HEAD_DIM + 1] for j in range(2)]

    def step(kt, carry, masked):
        k0 = pl.multiple_of(kt * t, t)
        kk = k_ref[0, pl.ds(k0, t), :]
        vv = v_ref[0, pl.ds(k0, t), :]
        new = []
        for j in range(2):
            m, l, acc = carry[3 * j:3 * j + 3]
            s = _dot_nt(qh[j], kk) + (cqh[j] - ck_ref[0, j, :, pl.ds(k0, t)])
            if masked:
                s = jnp.where(col <= row, s, -jnp.inf)
            m_new = jnp.maximum(m, jnp.max(s, axis=-1, keepdims=True))
            p = jnp.exp(s - m_new)
            alpha = jnp.exp(m - m_new)
            l = alpha * l + jnp.sum(p, axis=-1, keepdims=True)
            acc = alpha * acc + _dot(p.astype(BF), vv)
            new += [m_new, l, acc]
        return tuple(new)

    init = (jnp.full((t, 1), -jnp.inf, F32), jnp.zeros((t, 1), F32), jnp.zeros((t, LANES), F32)) * 2
    carry = lax.fori_loop(0, qi, lambda kt, c: step(kt, c, False), init)
    m0, l0, a0, m1, l1, a1 = step(qi, carry, True)
    o_ref[0] = jnp.where(lane < HEAD_DIM, a0 / l0, a1 / l1).astype(o_ref.dtype)


def _fox_prompt(qf, kfb, vfb, cexp, ck, t):
    b, s, _ = qf.shape
    qspec = pl.BlockSpec((1, t, LANES), lambda i, h, j: (i, j, h))
    kspec = pl.BlockSpec((1, s, LANES), lambda i, h, j: (i, 0, h))
    return pl.pallas_call(
        functools.partial(_fox_body, t=t),
        grid=(b, FOX_W // LANES, s // t),
        in_specs=[qspec, kspec, kspec, qspec, pl.BlockSpec((1, 2, 1, s), lambda i, h, j: (i, h, 0, 0))],
        out_specs=qspec,
        out_shape=jax.ShapeDtypeStruct((b, s, FOX_W), BF),
        compiler_params=pltpu.CompilerParams(dimension_semantics=("arbitrary",) * 3,
                                             vmem_limit_bytes=VMEM_LIMIT),
        name="fox_prompt",
    )(qf, kfb, vfb, cexp, ck)


def _sb_body(q_ref, k_ref, v_ref, u_ref, o_ref, *, t):
    qi = pl.program_id(2)
    q = q_ref[0].astype(F32)
    lane = lax.broadcasted_iota(jnp.int32, (t, LANES), 1)
    row = lax.broadcasted_iota(jnp.int32, (t, t), 0)
    col = lax.broadcasted_iota(jnp.int32, (t, t), 1)
    qh = [jnp.where(lane // HEAD_DIM == j, q, 0.0).astype(BF) for j in range(2)]

    def step(kt, carry, masked):
        k0 = pl.multiple_of(kt * t, t)
        kk = k_ref[0, pl.ds(k0, t), :]
        vv = v_ref[0, pl.ds(k0, t), :]
        new = []
        for j in range(2):
            run, acc = carry[2 * j:2 * j + 2]
            z = _dot_nt(qh[j], kk)
            lsn = _log_sigmoid(-z)
            lm = jnp.where(col < row, lsn, 0.0) if masked else lsn
            hi = lm.astype(BF)
            lo = (lm - hi.astype(F32)).astype(BF)
            incl = _dot(jnp.concatenate([hi, lo], axis=1), u_ref[...])
            w = jnp.exp(z + lsn + (incl - lm) + run)
            if masked:
                w = jnp.where(col < row, w, 0.0)
            new += [run + incl[:, 0:1], acc + _dot(w.astype(BF), vv)]
        return tuple(new)

    init = (jnp.zeros((t, 1), F32), jnp.zeros((t, LANES), F32)) * 2
    carry = step(qi, init, True)
    _, a0, _, a1 = lax.fori_loop(0, qi, lambda i, c: step(qi - 1 - i, c, False), carry)
    o_ref[0] = jnp.where(lane < HEAD_DIM, a0, a1).astype(o_ref.dtype)


def _sb_prompt(qs, ksb, vsb, t):
    b, s, _ = qs.shape
    qspec = pl.BlockSpec((1, t, LANES), lambda i, h, j: (i, j, h))
    kspec = pl.BlockSpec((1, s, LANES), lambda i, h, j: (i, 0, h))
    u = np.arange(t)[:, None] >= np.arange(t)[None, :]
    u2 = jnp.asarray(np.concatenate([u, u], axis=0), BF)
    return pl.pallas_call(
        functools.partial(_sb_body, t=t),
        grid=(b, SB_W // LANES, s // t),
        in_specs=[qspec, kspec, kspec, _const_spec((2 * t, t))],
        out_specs=qspec,
        out_shape=jax.ShapeDtypeStruct((b, s, SB_W), BF),
        compiler_params=pltpu.CompilerParams(dimension_semantics=("arbitrary",) * 3,
                                             vmem_limit_bytes=VMEM_LIMIT),
        name="sb_prompt",
    )(qs, ksb, vsb, u2)


def _memkv_body(mem_ref, nm_ref, wk_ref, wv_ref, gk_ref, hm_ref, mk_ref, mkb_ref, mv_ref, mvb_ref):
    mn = _rms(mem_ref[0], nm_ref[...]).astype(BF)
    mk = _head_norm(_dot(mn, wk_ref[...]), hm_ref[...], gk_ref[...])
    mv = _dot(mn, wv_ref[...])
    mk_ref[0] = mk
    mkb_ref[0] = mk.astype(BF)
    mv_ref[0] = mv
    mvb_ref[0] = mv.astype(BF)


def _mem_kv(mem, norm_mem, w_mem_k, w_mem_v, g_mem_k):
    b, n, _ = mem.shape
    spec = pl.BlockSpec((1, n, MEM_W), lambda i: (i, 0, 0))
    return pl.pallas_call(
        _memkv_body,
        grid=(b,),
        in_specs=[pl.BlockSpec((1, n, D_MODEL), lambda i: (i, 0, 0)), _const_spec((1, D_MODEL)),
                  _const_spec((D_MODEL, MEM_W)), _const_spec((D_MODEL, MEM_W)), _const_spec((1, MEM_W)),
                  _const_spec((MEM_W, MEM_W))],
        out_specs=[spec] * 4,
        out_shape=[jax.ShapeDtypeStruct((b, n, MEM_W), d) for d in (F32, BF, F32, BF)],
        compiler_params=pltpu.CompilerParams(dimension_semantics=("arbitrary",), vmem_limit_bytes=VMEM_LIMIT),
        name="mem_kv",
    )(mem, norm_mem.reshape(1, D_MODEL), w_mem_k.astype(BF), w_mem_v.astype(BF),
      jnp.tile(g_mem_k.reshape(1, HEAD_DIM), (1, H_MEM)), _head_mean_matrix(MEM_W))


def _memattn_body(q_ref, mk_ref, mv_ref, o_ref):
    q = q_ref[0].astype(F32)
    lane = lax.broadcasted_iota(jnp.int32, q.shape, 1)
    out = jnp.zeros(q.shape, F32)
    for h in range(H_MEM):
        qh = jnp.where(lane // HEAD_DIM == h, q, 0.0).astype(BF)
        s = _dot_nt(qh, mk_ref[0])
        e = jnp.exp(s - jnp.max(s, axis=-1, keepdims=True))
        p = e / jnp.sum(e, axis=-1, keepdims=True)
        out = jnp.where(lane // HEAD_DIM == h, _dot(p.astype(BF), mv_ref[0]), out)
    o_ref[0] = out.astype(o_ref.dtype)


def _mem_attend(qm, mkb, mvb, tm):
    b, s, _ = qm.shape
    n = mkb.shape[1]
    qspec = pl.BlockSpec((1, tm, MEM_W), lambda i, j: (i, j, 0))
    kspec = pl.BlockSpec((1, n, MEM_W), lambda i, j: (i, 0, 0))
    return pl.pallas_call(
        _memattn_body,
        grid=(b, s // tm),
        in_specs=[qspec, kspec, kspec],
        out_specs=qspec,
        out_shape=jax.ShapeDtypeStruct((b, s, MEM_W), BF),
        compiler_params=pltpu.CompilerParams(dimension_semantics=("arbitrary",) * 2,
                                             vmem_limit_bytes=VMEM_LIMIT),
        name="mem_attend",
    )(qm, mkb, mvb)


def _merge_body(x_ref, of_ref, os_ref, om_ref, na_ref, wg_ref, wuf_ref, wus_ref, wum_ref, wo_ref, nf_ref,
                wr_ref, br_ref, h_ref, hn_ref, eidx_ref, gate_ref):
    x = x_ref[...]
    tm = x.shape[0]
    xn = _rms(x, na_ref[...]).astype(BF)
    g = jax.nn.sigmoid(_dot(xn, wg_ref[...]))
    m = (g[:, :D_MODEL] * _dot(of_ref[...], wuf_ref[...])
         + g[:, D_MODEL:2 * D_MODEL] * _dot(os_ref[...], wus_ref[...])
         + g[:, 2 * D_MODEL:] * _dot(om_ref[...], wum_ref[...]))
    h = x + _dot(m.astype(BF), wo_ref[...])
    h_ref[...] = h
    hn = _rms(h, nf_ref[...]).astype(BF)
    hn_ref[...] = hn

    lt = _dot_nt(wr_ref[...], hn) + br_ref[...]
    grp = [lt[i:i + 1] for i in range(N_GROUPS)]
    gmax = functools.reduce(jnp.maximum, grp)
    gidx = jnp.where(grp[0] == gmax, 0, jnp.where(grp[1] == gmax, 1, jnp.where(grp[2] == gmax, 2, 3)))
    g_gate = 1.0 / functools.reduce(jnp.add, [jnp.exp(v - gmax) for v in grp])
    ins = []
    for k in range(EXPERTS_PER_GROUP):
        v = jnp.zeros((1, tm), F32)
        for gi in range(N_GROUPS):
            r = N_GROUPS + gi * EXPERTS_PER_GROUP + k
            v = jnp.where(gidx == gi, lt[r:r + 1], v)
        ins.append(v)
    imax = functools.reduce(jnp.maximum, ins)
    ex = [jnp.exp(v - imax) for v in ins]
    tot = functools.reduce(jnp.add, ex)
    p = [v / tot for v in ex]

    def top1(vals):
        best = functools.reduce(jnp.maximum, vals)
        idx = jnp.where(vals[0] == best, 0, jnp.where(vals[1] == best, 1, jnp.where(vals[2] == best, 2, 3)))
        return best, idx

    p1, i1 = top1(p)
    p2, i2 = top1([jnp.where(i1 == k, -1.0, p[k]) for k in range(EXPERTS_PER_GROUP)])
    den = p1 + p2
    row = lax.broadcasted_iota(jnp.int32, (8, tm), 0)
    e0 = gidx * EXPERTS_PER_GROUP + i1
    e1 = gidx * EXPERTS_PER_GROUP + i2
    eidx_ref[...] = jnp.where(row == 0, e0, jnp.where(row == 1, e1, 0))
    gate_ref[...] = jnp.where(row == 0, g_gate * p1 / den, jnp.where(row == 1, g_gate * p2 / den, 0.0))


def _merge(x, o_fox, o_sb, o_mem, norm_attn, wg, w_up_fox, w_up_sb, w_up_mem, w_o, norm_ffn, wr, br, tm):
    t = x.shape[0]
    row = lambda w: pl.BlockSpec((tm, w), lambda i: (i, 0))
    col = pl.BlockSpec((8, tm), lambda i: (0, i))
    return pl.pallas_call(
        _merge_body,
        grid=(t // tm,),
        in_specs=[row(D_MODEL), row(FOX_W), row(SB_W), row(MEM_W), _const_spec((1, D_MODEL)),
                  _const_spec((D_MODEL, 3 * D_MODEL)), _const_spec((FOX_W, D_MODEL)), _const_spec((SB_W, D_MODEL)),
                  _const_spec((MEM_W, D_MODEL)), _const_spec((D_MODEL, D_MODEL)), _const_spec((1, D_MODEL)),
                  _const_spec((32, D_MODEL)), _const_spec((32, 1))],
        out_specs=[row(D_MODEL), row(D_MODEL), col, col],
        out_shape=[jax.ShapeDtypeStruct((t, D_MODEL), F32), jax.ShapeDtypeStruct((t, D_MODEL), BF),
                   jax.ShapeDtypeStruct((8, t), jnp.int32), jax.ShapeDtypeStruct((8, t), F32)],
        compiler_params=pltpu.CompilerParams(dimension_semantics=("arbitrary",), vmem_limit_bytes=VMEM_LIMIT),
        name="merge",
    )(x, o_fox, o_sb, o_mem, norm_attn.reshape(1, D_MODEL), wg, w_up_fox, w_up_sb, w_up_mem, w_o,
      norm_ffn.reshape(1, D_MODEL), wr, br)


def _moe_body(cnt_ref, hn_ref, h_ref, irow_ref, icol_ref, gcol_ref, w1_ref, w3_ref, w2_ref, su_ref, sl_ref,
              y_ref, rrow_ref, rcol_ref, *, tt, r):
    i = pl.program_id(0)
    e = pl.program_id(1)

    @pl.when(e == 0)
    def _():
        y_ref[...] = h_ref[...]
        erow = lax.broadcasted_iota(jnp.int32, (N_EXPERTS, tt), 0)
        a_row = ((erow == irow_ref[0:1, :]) | (erow == irow_ref[1:2, :])).astype(BF)
        ranks = _dot(a_row, su_ref[...])
        for k in range(N_EXPERTS):
            rrow_ref[k] = ranks[k:k + 1, :]
        lane = lax.broadcasted_iota(jnp.int32, (tt, LANES), 1)
        a_col = ((lane == icol_ref[:, 0:1]) | (lane == icol_ref[:, 1:2])).astype(BF)
        rcol_ref[...] = _dot(sl_ref[...], a_col)

    n = cnt_ref[i * N_EXPERTS + e]
    sel_row = jnp.where((irow_ref[0:1, :] == e) | (irow_ref[1:2, :] == e), rrow_ref[e], -1.0)
    lane = lax.broadcasted_iota(jnp.int32, (tt, LANES), 1)
    rank_col = jnp.sum(jnp.where(lane == e, rcol_ref[...], 0.0), axis=1, keepdims=True)
    a0 = icol_ref[:, 0:1] == e
    a1 = icol_ref[:, 1:2] == e
    sel_col = jnp.where(a0 | a1, rank_col, -1.0)
    gate_col = jnp.where(a0, gcol_ref[:, 0:1], jnp.where(a1, gcol_ref[:, 1:2], 0.0))

    def chunk(c, carry):
        base = (c * r).astype(F32)
        p = (sel_row - base == lax.broadcasted_iota(jnp.int32, (r, tt), 0).astype(F32)).astype(BF)
        pt = (sel_col - base == lax.broadcasted_iota(jnp.int32, (tt, r), 1).astype(F32)).astype(BF)
        xc = _dot(p, hn_ref[...]).astype(BF)
        hid = jax.nn.silu(_dot(xc, w1_ref[0])) * _dot(xc, w3_ref[0])
        out = _dot(hid.astype(BF), w2_ref[0])
        y_ref[...] += gate_col * _dot(pt, out.astype(BF))
        return carry

    lax.fori_loop(0, (n + r - 1) // r, chunk, 0)


def _moe(hn, h, eidx, gates, w1, w3, w2, tt, r):
    t = hn.shape[0]
    nt = t // tt
    ids = eidx[:2]
    onehot = (ids[:, :, None] == jnp.arange(N_EXPERTS)[None, None, :]).any(axis=0)
    cnt = onehot.reshape(nt, tt, N_EXPERTS).sum(axis=1).astype(jnp.int32).reshape(-1)
    icol = jnp.transpose(eidx)
    gcol = jnp.transpose(gates)
    pos = np.arange(tt)
    su = jnp.asarray(pos[:, None] < pos[None, :], BF)
    sl = jnp.asarray(pos[:, None] > pos[None, :], BF)
    tok = lambda w: pl.BlockSpec((tt, w), lambda i, e, c: (i, 0))
    wspec = lambda a, b: pl.BlockSpec((1, a, b), lambda i, e, c: (e, 0, 0))
    return pl.pallas_call(
        functools.partial(_moe_body, tt=tt, r=r),
        grid_spec=pltpu.PrefetchScalarGridSpec(
            num_scalar_prefetch=1,
            grid=(nt, N_EXPERTS),
            in_specs=[tok(D_MODEL), tok(D_MODEL), pl.BlockSpec((8, tt), lambda i, e, c: (0, i)), tok(8), tok(8),
                      wspec(D_MODEL, D_EXPERT), wspec(D_MODEL, D_EXPERT), wspec(D_EXPERT, D_MODEL),
                      pl.BlockSpec((tt, tt), lambda i, e, c: (0, 0)), pl.BlockSpec((tt, tt), lambda i, e, c: (0, 0))],
            out_specs=tok(D_MODEL),
            scratch_shapes=[pltpu.VMEM((N_EXPERTS, 1, tt), F32), pltpu.VMEM((tt, LANES), F32)]),
        out_shape=jax.ShapeDtypeStruct((t, D_MODEL), F32),
        compiler_params=pltpu.CompilerParams(dimension_semantics=("arbitrary",) * 2,
                                             vmem_limit_bytes=VMEM_LIMIT),
        name="moe",
    )(cnt, hn, h, eidx, icol, gcol, w1, w3, w2, su, sl)


def _block_diag_rows(q_row, width):
    sub = lax.broadcasted_iota(jnp.int32, (8, width), 0)
    lane = lax.broadcasted_iota(jnp.int32, (8, width), 1)
    return jnp.where(lane // HEAD_DIM == sub, jnp.broadcast_to(q_row.astype(F32), (8, width)), 0.0).astype(BF)


def _diag_heads(acc):
    sub = lax.broadcasted_iota(jnp.int32, acc.shape, 0)
    lane = lax.broadcasted_iota(jnp.int32, acc.shape, 1)
    return jnp.sum(jnp.where(lane // HEAD_DIM == sub, acc, 0.0), axis=0, keepdims=True)


def _decode_body(pt_ref, qf_ref, kfn_ref, vfn_ref, lfn_ref, qs_ref, qm_ref, mk_ref, mv_ref, su_ref, *rest, g):
    fk, fv, lf, sk, sv = (rest[i * g:(i + 1) * g] for i in range(5))
    of_ref, os_ref, om_ref, m_ref, l_ref, acc_ref, cf_ref, run_ref, accs_ref = rest[5 * g:]
    j = pl.program_id(1)
    qbd = _block_diag_rows(qf_ref[0], FOX_W)
    qsbd = _block_diag_rows(qs_ref[0], SB_W)

    @pl.when(j == 0)
    def _():
        m_ref[...] = jnp.sum(qbd.astype(F32) * kfn_ref[0].astype(F32), axis=-1, keepdims=True)
        l_ref[...] = jnp.ones_like(l_ref)
        acc_ref[...] = jnp.broadcast_to(vfn_ref[0].astype(F32), acc_ref.shape)
        cf_ref[...] = lfn_ref[0]
        run_ref[...] = jnp.zeros_like(run_ref)
        accs_ref[...] = jnp.zeros_like(accs_ref)

    for i in range(g):
        lfp = lf[i][0]
        suf = _exact_right(lfp, su_ref[...])
        cf = cf_ref[...]
        s = _dot(qbd, fk[i][0].astype(BF)) + suf + cf
        m = m_ref[...]
        m_new = jnp.maximum(m, jnp.max(s, axis=-1, keepdims=True))
        p = jnp.exp(s - m_new)
        alpha = jnp.exp(m - m_new)
        l_ref[...] = alpha * l_ref[...] + jnp.sum(p, axis=-1, keepdims=True)
        acc_ref[...] = alpha * acc_ref[...] + _dot_nt(p.astype(BF), fv[i][0].astype(BF))
        m_ref[...] = m_new
        cf_ref[...] = cf + suf[:, 0:1] + lfp[:, 0:1]

        z = _dot(qsbd, sk[i][0].astype(BF))
        lsn = _log_sigmoid(-z)
        sufs = _exact_right(lsn, su_ref[...])
        run = run_ref[...]
        w = jnp.exp(z + lsn + sufs + run)
        accs_ref[...] += _dot_nt(w.astype(BF), sv[i][0].astype(BF))
        run_ref[...] = run + sufs[:, 0:1] + lsn[:, 0:1]

    @pl.when(j == pl.num_programs(1) - 1)
    def _():
        of_ref[0] = _diag_heads(acc_ref[...] / l_ref[...]).astype(of_ref.dtype)
        os_ref[0] = _diag_heads(accs_ref[...]).astype(os_ref.dtype)
        s = _dot(_block_diag_rows(qm_ref[0], MEM_W), mk_ref[0].astype(BF))
        e = jnp.exp(s - jnp.max(s, axis=-1, keepdims=True))
        p = e / jnp.sum(e, axis=-1, keepdims=True)
        om_ref[0] = _diag_heads(_dot_nt(p.astype(BF), mv_ref[0].astype(BF))).astype(om_ref.dtype)


def _decode_attend(page_table, qf, kfn, vfn, lfn, qs, qm, fk, fv, lfc, sk, sv, mk, mv, g):
    nb, npg = page_table.shape
    page = fk.shape[2]
    per_sample = lambda a, b: pl.BlockSpec((1, a, b), lambda i, j, pt: (i, 0, 0))

    def paged(rows, gi):
        return pl.BlockSpec((1, rows, page), lambda i, j, pt: (pt[i * npg + npg - 1 - (j * g + gi)], 0, 0))

    pos = np.arange(page)
    su = jnp.asarray(pos[:, None] > pos[None, :], BF)
    in_specs = [per_sample(1, FOX_W), per_sample(1, FOX_W), per_sample(1, FOX_W), per_sample(8, 1),
                per_sample(1, SB_W), per_sample(1, MEM_W), per_sample(MEM_W, mk.shape[2]),
                per_sample(MEM_W, mk.shape[2]), pl.BlockSpec((page, page), lambda i, j, pt: (0, 0))]
    operands = [qf, kfn, vfn, lfn, qs, qm, mk, mv, su]
    for arr, rows in ((fk, FOX_W), (fv, FOX_W), (lfc, H_FOX), (sk, SB_W), (sv, SB_W)):
        in_specs += [paged(rows, gi) for gi in range(g)]
        operands += [arr] * g
    return pl.pallas_call(
        functools.partial(_decode_body, g=g),
        grid_spec=pltpu.PrefetchScalarGridSpec(
            num_scalar_prefetch=1,
            grid=(nb, npg // g),
            in_specs=in_specs,
            out_specs=[per_sample(1, FOX_W), per_sample(1, SB_W), per_sample(1, MEM_W)],
            scratch_shapes=[pltpu.VMEM((8, 1), F32), pltpu.VMEM((8, 1), F32), pltpu.VMEM((8, FOX_W), F32),
                            pltpu.VMEM((8, 1), F32), pltpu.VMEM((8, 1), F32), pltpu.VMEM((8, SB_W), F32)]),
        out_shape=[jax.ShapeDtypeStruct((nb, 1, w), BF) for w in (FOX_W, SB_W, MEM_W)],
        compiler_params=pltpu.CompilerParams(dimension_semantics=("arbitrary",) * 2,
                                             vmem_limit_bytes=VMEM_LIMIT),
        name="decode_attend",
    )(page_table.reshape(-1), *operands)


def _router_params(w_grp, b_grp, w_exp, b_exp):
    n = N_GROUPS + N_EXPERTS
    wr = jnp.zeros((32, D_MODEL), BF).at[:n].set(jnp.concatenate([w_grp, w_exp], axis=1).T.astype(BF))
    br = jnp.zeros((32, 1), F32).at[:n, 0].set(jnp.concatenate([b_grp, b_exp]))
    return wr, br


def _split_w_in(w_in):
    o1 = 3 * FOX_W
    o2 = o1 + H_FOX
    o3 = o2 + 3 * SB_W + MEM_W
    wa = jnp.concatenate([w_in[:, :o1], w_in[:, o2:o3]], axis=1).astype(BF)
    wf = jnp.zeros((D_MODEL, LANES), BF).at[:, :H_FOX].set(w_in[:, o1:o2].astype(BF))
    return wa, wf, w_in[:, o3:].astype(BF)


def _position_minor(cache):
    pool, page, h, dh = cache.shape
    return jnp.transpose(cache, (0, 2, 3, 1)).reshape(pool, h * dh, page)


def kernel(x_prompt, x_sample, mem_prompt, cache_fox_k, cache_fox_v, cache_fox_logf, cache_sb_k, cache_sb_v,
           cache_mem_k, cache_mem_v, page_table, norm_attn, w_in, b_forget, g_fox_q, g_fox_k, g_mem_q, g_mem_k,
           norm_mem, w_mem_k, w_mem_v, w_up_fox, w_up_sb, w_up_mem, w_o, norm_ffn, w_grp, b_grp, w_exp, b_exp,
           w1, w3, w2):
    assert w_in.shape[0] == 1, "one layer"
    b, s, _ = x_prompt.shape
    nb = x_sample.shape[0]
    wa, wf, wg = _split_w_in(w_in[0])
    wr, br = _router_params(w_grp[0], b_grp[0], w_exp[0], b_exp[0])
    merge_w = (norm_attn[0], wg, w_up_fox[0].astype(BF), w_up_sb[0].astype(BF), w_up_mem[0].astype(BF),
               w_o[0].astype(BF), norm_ffn[0], wr, br)
    moe_w = (w1[0].astype(BF), w3[0].astype(BF), w2[0].astype(BF))
    proj_w = (norm_attn[0], wa, wf, b_forget[0], g_fox_q[0], g_fox_k[0], g_mem_q[0])

    qf, kf, kfb, vf, vfb, lf, cexp, qs, ks, ksb, vs, vsb, qm = _inproj(x_prompt, *proj_w, tm=PROMPT_TM)
    ck = jnp.swapaxes(cexp[..., ::HEAD_DIM], 1, 2).reshape(b, H_FOX, 1, s)
    o_f = _fox_prompt(qf, kfb, vfb, cexp, ck, t=ATTN_T)
    o_s = _sb_prompt(qs, ksb, vsb, t=ATTN_T)
    mk, mkb, mv, mvb = _mem_kv(mem_prompt, norm_mem[0], w_mem_k[0], w_mem_v[0], g_mem_k[0])
    o_m = _mem_attend(qm, mkb, mvb, tm=PROMPT_TM)
    t = b * s
    flat = lambda a: a.reshape(t, a.shape[-1])
    h, hn, eidx, gates = _merge(flat(x_prompt), flat(o_f), flat(o_s), flat(o_m), *merge_w, tm=PROMPT_TM)
    y_prompt = _moe(hn, h, eidx, gates, *moe_w, tt=MOE_TT, r=MOE_R).reshape(b, s, D_MODEL)

    sq = _inproj(x_sample.reshape(1, nb, D_MODEL), *proj_w, tm=nb)
    qf2, kf2, kfb2, vf2, vfb2, lf2, _, qs2, ks2, _, vs2, _, qm2 = sq
    per = lambda a: a.reshape(nb, 1, a.shape[-1])
    o_f2, o_s2, o_m2 = _decode_attend(
        page_table, per(qf2), per(kfb2), per(vfb2), lf2[0, :, :H_FOX].reshape(nb, H_FOX, 1), per(qs2), per(qm2),
        _position_minor(cache_fox_k[0]), _position_minor(cache_fox_v[0]),
        jnp.transpose(cache_fox_logf[0], (0, 2, 1)), _position_minor(cache_sb_k[0]), _position_minor(cache_sb_v[0]),
        _position_minor(cache_mem_k[0]), _position_minor(cache_mem_v[0]), g=DECODE_PAGES_PER_STEP)
    flat2 = lambda a: a.reshape(nb, a.shape[-1])
    h2, hn2, eidx2, gates2 = _merge(flat2(x_sample), flat2(o_f2), flat2(o_s2), flat2(o_m2), *merge_w, tm=nb)
    y_sample = _moe(hn2, h2, eidx2, gates2, *moe_w, tt=nb, r=nb).reshape(nb, 1, D_MODEL)

    heads = lambda a, n: a.reshape(1, a.shape[0], a.shape[1], n, HEAD_DIM)
    dec = lambda a, n: a.reshape(1, nb, 1, n, HEAD_DIM)
    return (y_prompt, y_sample,
            heads(kf, H_FOX), heads(vf, H_FOX), lf[..., :H_FOX].reshape(1, b, s, H_FOX),
            heads(ks, H_SB), heads(vs, H_SB), heads(mk, H_MEM), heads(mv, H_MEM),
            dec(kf2, H_FOX), dec(vf2, H_FOX), lf2[..., :H_FOX].reshape(1, nb, 1, H_FOX),
            dec(ks2, H_SB), dec(vs2, H_SB))
```

```python
import functools

import numpy as np
import jax
import jax.numpy as jnp
from jax import lax
from jax.experimental import pallas as pl
from jax.experimental.pallas import tpu as pltpu

BF = jnp.bfloat16
F32 = jnp.float32

D_MODEL = 1024
HEAD_DIM = 64
H_FOX = 8
H_SB = 4
H_MEM = 4
FOX_W = H_FOX * HEAD_DIM
SB_W = H_SB * HEAD_DIM
MEM_W = H_MEM * HEAD_DIM
N_GROUPS = 4
EXPERTS_PER_GROUP = 4
N_EXPERTS = N_GROUPS * EXPERTS_PER_GROUP
D_EXPERT = 512
EPS = 1e-6
ATTN_SCALE = HEAD_DIM ** -0.5
LANES = 128
A_W = 3 * FOX_W + 3 * SB_W + MEM_W
VMEM_LIMIT = 56 * 1024 * 1024
PROMPT_TM = 256
ATTN_T = 256
MOE_TT = 1024
FOX_TK = 512
MOE_R = 256
DECODE_PAGES_PER_STEP = 8


def _dot(a, b):
    return jnp.dot(a, b, preferred_element_type=F32)


def _dot_nt(a, b):
    return lax.dot_general(a, b, (((1,), (1,)), ((), ())), preferred_element_type=F32)


def _split3(a):
    a1 = a.astype(BF)
    r = a - a1.astype(F32)
    a2 = r.astype(BF)
    a3 = (r - a2.astype(F32)).astype(BF)
    return a1, a2, a3


def _exact_left(m, b):
    b1, b2, b3 = _split3(b)
    return _dot(m, b1) + _dot(m, b2) + _dot(m, b3)


def _exact_right(a, m):
    a1, a2, a3 = _split3(a)
    return _dot(a1, m) + _dot(a2, m) + _dot(a3, m)


def _suffix_sums(a, later):
    hi = a.astype(BF)
    lo = (a - hi.astype(F32)).astype(BF)
    return _dot(hi, later) + _dot(lo, later)


def _log_sigmoid(x):
    return jnp.minimum(x, 0.0) - jnp.log1p(jnp.exp(-jnp.abs(x)))


def _rms(x, g):
    return x * lax.rsqrt(jnp.mean(x * x, axis=-1, keepdims=True) + EPS) * g


def _head_norm(t, head_mean, g):
    ms = _dot((t * t).astype(BF), head_mean)
    return t * lax.rsqrt(ms + EPS) * g


def _const_spec(shape):
    return pl.BlockSpec(shape, lambda *_: (0,) * len(shape))


def _head_mean_matrix(width):
    i = np.arange(width)
    return jnp.asarray((i[:, None] // HEAD_DIM == i[None, :] // HEAD_DIM) / HEAD_DIM, BF)


def _inproj_body(x_ref, na_ref, wa_ref, wf_ref, bfg_ref, gfq_ref, gfk_ref, gmq_ref, hm512_ref, hm256_ref,
                 tri_ref, exp_ref,
                 qf_ref, kf_ref, kfb_ref, vf_ref, vfb_ref, lf_ref, cexp_ref, qs_ref, ks_ref, ksb_ref,
                 vs_ref, vsb_ref, qm_ref, carry_ref):
    tm = x_ref.shape[1]

    @pl.when(pl.program_id(1) == 0)
    def _():
        carry_ref[...] = jnp.zeros_like(carry_ref)

    xn = _rms(x_ref[0], na_ref[...]).astype(BF)
    y = _dot(xn, wa_ref[...])
    f = _dot(xn, wf_ref[...])

    qf = _head_norm(y[:, 0:FOX_W], hm512_ref[...], gfq_ref[...])
    qf_ref[0] = (qf * ATTN_SCALE).astype(BF)
    kf = _head_norm(y[:, FOX_W:2 * FOX_W], hm512_ref[...], gfk_ref[...])
    kf_ref[0] = kf
    kfb_ref[0] = kf.astype(BF)
    vf = y[:, 2 * FOX_W:3 * FOX_W]
    vf_ref[0] = vf
    vfb_ref[0] = vf.astype(BF)
    o = 3 * FOX_W
    qs_ref[0] = (y[:, o:o + SB_W] * ATTN_SCALE).astype(BF)
    ks = y[:, o + SB_W:o + 2 * SB_W]
    ks_ref[0] = ks
    ksb_ref[0] = ks.astype(BF)
    vs = y[:, o + 2 * SB_W:o + 3 * SB_W]
    vs_ref[0] = vs
    vsb_ref[0] = vs.astype(BF)
    qm = _head_norm(y[:, o + 3 * SB_W:o + 3 * SB_W + MEM_W], hm256_ref[...], gmq_ref[...])
    qm_ref[0] = (qm * ATTN_SCALE).astype(BF)

    lane = lax.broadcasted_iota(jnp.int32, (tm, LANES), 1)
    lf = jnp.where(lane < H_FOX, _log_sigmoid(f + bfg_ref[...]), 0.0)
    lf_ref[0] = lf
    c = carry_ref[...] + _exact_left(tri_ref[...], lf)
    carry_ref[...] = c[tm - 1:tm, :]
    cexp_ref[0] = _exact_right(c, exp_ref[...])


def _inproj(x, norm_attn, wa, wf, b_forget, g_fox_q, g_fox_k, g_mem_q, tm):
    b, s, _ = x.shape
    nt = s // tm
    tile = lambda w: pl.BlockSpec((1, tm, w), lambda i, j: (i, j, 0))
    rep = lambda g, n: jnp.tile(g.reshape(1, HEAD_DIM), (1, n))
    bfg = jnp.zeros((1, LANES), F32).at[0, :H_FOX].set(b_forget)
    tri = jnp.asarray(np.tril(np.ones((tm, tm))), BF)
    i = np.arange(FOX_W)
    expand = jnp.asarray(np.arange(LANES)[:, None] == i[None, :] // HEAD_DIM, BF)
    shapes = [(FOX_W, BF), (FOX_W, F32), (FOX_W, BF), (FOX_W, F32), (FOX_W, BF), (LANES, F32), (FOX_W, F32),
              (SB_W, BF), (SB_W, F32), (SB_W, BF), (SB_W, F32), (SB_W, BF), (MEM_W, BF)]
    return pl.pallas_call(
        _inproj_body,
        grid=(b, nt),
        in_specs=[tile(D_MODEL), _const_spec((1, D_MODEL)), _const_spec((D_MODEL, A_W)),
                  _const_spec((D_MODEL, LANES)), _const_spec((1, LANES)), _const_spec((1, FOX_W)),
                  _const_spec((1, FOX_W)), _const_spec((1, MEM_W)), _const_spec((FOX_W, FOX_W)),
                  _const_spec((MEM_W, MEM_W)), _const_spec((tm, tm)), _const_spec((LANES, FOX_W))],
        out_specs=[tile(w) for w, _ in shapes],
        out_shape=[jax.ShapeDtypeStruct((b, s, w), d) for w, d in shapes],
        scratch_shapes=[pltpu.VMEM((1, LANES), F32)],
        compiler_params=pltpu.CompilerParams(dimension_semantics=("arbitrary", "arbitrary"),
                                             vmem_limit_bytes=VMEM_LIMIT),
        name="inproj",
    )(x, norm_attn.reshape(1, D_MODEL), wa, wf, bfg, rep(g_fox_q, H_FOX), rep(g_fox_k, H_FOX),
      rep(g_mem_q, H_MEM), _head_mean_matrix(FOX_W), _head_mean_matrix(MEM_W), tri, expand)


def _head_masked(q, n_heads):
    t = q.shape[0]
    lane = lax.broadcasted_iota(jnp.int32, (t, LANES), 1)
    out = []
    for h in range(n_heads):
        pair = q[:, (h // 2) * LANES:(h // 2 + 1) * LANES]
        out.append(jnp.where(lane // HEAD_DIM == h % 2, pair, 0.0).astype(BF))
    return out


def _merge_pairs(per_head):
    t = per_head[0].shape[0]
    lane = lax.broadcasted_iota(jnp.int32, (t, LANES), 1)
    pairs = [jnp.where(lane < HEAD_DIM, per_head[h], per_head[h + 1]) for h in range(0, len(per_head), 2)]
    return jnp.concatenate(pairs, axis=1)


def _fox_body(q_ref, k_ref, v_ref, cq_ref, ck_ref, o_ref, *, tq, tk):
    qi = pl.program_id(1)
    row = lax.broadcasted_iota(jnp.int32, (tq, tk), 0)
    col = lax.broadcasted_iota(jnp.int32, (tq, tk), 1)
    qh = _head_masked(q_ref[0].astype(F32), H_FOX)
    cq = cq_ref[0]
    cqh = [jnp.broadcast_to(cq[:, h * HEAD_DIM:h * HEAD_DIM + 1], (tq, tk)) for h in range(H_FOX)]

    def step(kt, carry, masked):
        k0 = pl.multiple_of(kt * tk, tk)
        new = []
        for h in range(H_FOX):
            lanes = pl.ds((h // 2) * LANES, LANES)
            m, l, acc = carry[3 * h:3 * h + 3]
            s = _dot_nt(qh[h], k_ref[0, pl.ds(k0, tk), lanes]) + (cqh[h] - ck_ref[0, h, :, pl.ds(k0, tk)])
            if masked:
                s = jnp.where(k0 + col <= qi * tq + row, s, -jnp.inf)
            m_new = jnp.maximum(m, jnp.max(s, axis=-1, keepdims=True))
            p = jnp.exp(s - m_new)
            alpha = jnp.exp(m - m_new)
            l = alpha * l + jnp.sum(p, axis=-1, keepdims=True)
            acc = alpha * acc + _dot(p.astype(BF), v_ref[0, pl.ds(k0, tk), lanes])
            new += [m_new, l, acc]
        return tuple(new)

    init = (jnp.full((tq, 1), -jnp.inf, F32), jnp.zeros((tq, 1), F32), jnp.zeros((tq, LANES), F32)) * H_FOX
    n_full = (qi * tq) // tk
    carry = lax.fori_loop(0, n_full, lambda kt, c: step(kt, c, False), init)
    fin = step(n_full, carry, True)
    o_ref[0] = _merge_pairs([fin[3 * h + 2] / fin[3 * h + 1] for h in range(H_FOX)]).astype(o_ref.dtype)


def _fox_prompt(qf, kfb, vfb, cexp, ck, tq, tk):
    b, s, _ = qf.shape
    assert tk % tq == 0 and s % tk == 0
    qspec = pl.BlockSpec((1, tq, FOX_W), lambda i, j: (i, j, 0))
    kspec = pl.BlockSpec((1, s, FOX_W), lambda i, j: (i, 0, 0))
    return pl.pallas_call(
        functools.partial(_fox_body, tq=tq, tk=tk),
        grid=(b, s // tq),
        in_specs=[qspec, kspec, kspec, qspec, pl.BlockSpec((1, H_FOX, 1, s), lambda i, j: (i, 0, 0, 0))],
        out_specs=qspec,
        out_shape=jax.ShapeDtypeStruct((b, s, FOX_W), BF),
        compiler_params=pltpu.CompilerParams(dimension_semantics=("arbitrary",) * 2,
                                             vmem_limit_bytes=VMEM_LIMIT),
        name="fox_prompt",
    )(qf, kfb, vfb, cexp, ck)


def _sb_body(q_ref, k_ref, v_ref, u_ref, o_ref, *, t):
    qi = pl.program_id(1)
    row = lax.broadcasted_iota(jnp.int32, (t, t), 0)
    col = lax.broadcasted_iota(jnp.int32, (t, t), 1)
    qh = _head_masked(q_ref[0].astype(F32), H_SB)

    def step(kt, carry, masked):
        k0 = pl.multiple_of(kt * t, t)
        new = []
        for h in range(H_SB):
            lanes = pl.ds((h // 2) * LANES, LANES)
            vv = v_ref[0, pl.ds(k0, t), lanes]
            run, acc = carry[2 * h:2 * h + 2]
            z = _dot_nt(qh[h], k_ref[0, pl.ds(k0, t), lanes])
            lsn = jnp.minimum(-z, 0.0) - jnp.log(1.0 + jnp.exp(-jnp.abs(z)))
            lm = jnp.where(col < row, lsn, 0.0) if masked else lsn
            hi = lm.astype(BF)
            lo = (lm - hi.astype(F32)).astype(BF)
            incl = _dot(jnp.concatenate([hi, lo], axis=1), u_ref[...])
            w = jnp.exp(z + lsn + (incl - lm) + run)
            if masked:
                w = jnp.where(col < row, w, 0.0)
            new += [run + incl[:, 0:1], acc + _dot(w.astype(BF), vv)]
        return tuple(new)

    init = (jnp.zeros((t, 1), F32), jnp.zeros((t, LANES), F32)) * H_SB
    carry = step(qi, init, True)
    fin = lax.fori_loop(0, qi, lambda i, c: step(qi - 1 - i, c, False), carry)
    o_ref[0] = _merge_pairs([fin[2 * h + 1] for h in range(H_SB)]).astype(o_ref.dtype)


def _sb_prompt(qs, ksb, vsb, t):
    b, s, _ = qs.shape
    qspec = pl.BlockSpec((1, t, SB_W), lambda i, j: (i, j, 0))
    kspec = pl.BlockSpec((1, s, SB_W), lambda i, j: (i, 0, 0))
    u = np.arange(t)[:, None] >= np.arange(t)[None, :]
    u2 = jnp.asarray(np.concatenate([u, u], axis=0), BF)
    return pl.pallas_call(
        functools.partial(_sb_body, t=t),
        grid=(b, s // t),
        in_specs=[qspec, kspec, kspec, _const_spec((2 * t, t))],
        out_specs=qspec,
        out_shape=jax.ShapeDtypeStruct((b, s, SB_W), BF),
        compiler_params=pltpu.CompilerParams(dimension_semantics=("arbitrary",) * 2,
                                             vmem_limit_bytes=VMEM_LIMIT),
        name="sb_prompt",
    )(qs, ksb, vsb, u2)


def _memkv_body(mem_ref, nm_ref, wk_ref, wv_ref, gk_ref, hm_ref, mk_ref, mkb_ref, mv_ref, mvb_ref):
    mn = _rms(mem_ref[0], nm_ref[...]).astype(BF)
    mk = _head_norm(_dot(mn, wk_ref[...]), hm_ref[...], gk_ref[...])
    mv = _dot(mn, wv_ref[...])
    mk_ref[0] = mk
    mkb_ref[0] = mk.astype(BF)
    mv_ref[0] = mv
    mvb_ref[0] = mv.astype(BF)


def _mem_kv(mem, norm_mem, w_mem_k, w_mem_v, g_mem_k):
    b, n, _ = mem.shape
    spec = pl.BlockSpec((1, n, MEM_W), lambda i: (i, 0, 0))
    return pl.pallas_call(
        _memkv_body,
        grid=(b,),
        in_specs=[pl.BlockSpec((1, n, D_MODEL), lambda i: (i, 0, 0)), _const_spec((1, D_MODEL)),
                  _const_spec((D_MODEL, MEM_W)), _const_spec((D_MODEL, MEM_W)), _const_spec((1, MEM_W)),
                  _const_spec((MEM_W, MEM_W))],
        out_specs=[spec] * 4,
        out_shape=[jax.ShapeDtypeStruct((b, n, MEM_W), d) for d in (F32, BF, F32, BF)],
        compiler_params=pltpu.CompilerParams(dimension_semantics=("arbitrary",), vmem_limit_bytes=VMEM_LIMIT),
        name="mem_kv",
    )(mem, norm_mem.reshape(1, D_MODEL), w_mem_k.astype(BF), w_mem_v.astype(BF),
      jnp.tile(g_mem_k.reshape(1, HEAD_DIM), (1, H_MEM)), _head_mean_matrix(MEM_W))


def _memattn_body(q_ref, mk_ref, mv_ref, o_ref):
    q = q_ref[0].astype(F32)
    lane = lax.broadcasted_iota(jnp.int32, q.shape, 1)
    out = jnp.zeros(q.shape, F32)
    for h in range(H_MEM):
        qh = jnp.where(lane // HEAD_DIM == h, q, 0.0).astype(BF)
        s = _dot_nt(qh, mk_ref[0])
        e = jnp.exp(s - jnp.max(s, axis=-1, keepdims=True))
        p = e / jnp.sum(e, axis=-1, keepdims=True)
        out = jnp.where(lane // HEAD_DIM == h, _dot(p.astype(BF), mv_ref[0]), out)
    o_ref[0] = out.astype(o_ref.dtype)


def _mem_attend(qm, mkb, mvb, tm):
    b, s, _ = qm.shape
    n = mkb.shape[1]
    qspec = pl.BlockSpec((1, tm, MEM_W), lambda i, j: (i, j, 0))
    kspec = pl.BlockSpec((1, n, MEM_W), lambda i, j: (i, 0, 0))
    return pl.pallas_call(
        _memattn_body,
        grid=(b, s // tm),
        in_specs=[qspec, kspec, kspec],
        out_specs=qspec,
        out_shape=jax.ShapeDtypeStruct((b, s, MEM_W), BF),
        compiler_params=pltpu.CompilerParams(dimension_semantics=("arbitrary",) * 2,
                                             vmem_limit_bytes=VMEM_LIMIT),
        name="mem_attend",
    )(qm, mkb, mvb)


def _merge_body(x_ref, of_ref, os_ref, om_ref, na_ref, wg_ref, wuf_ref, wus_ref, wum_ref, wo_ref, nf_ref,
                wr_ref, br_ref, h_ref, hn_ref, eidx_ref, gate_ref):
    x = x_ref[...]
    tm = x.shape[0]
    xn = _rms(x, na_ref[...]).astype(BF)
    g = jax.nn.sigmoid(_dot(xn, wg_ref[...]))
    m = (g[:, :D_MODEL] * _dot(of_ref[...], wuf_ref[...])
         + g[:, D_MODEL:2 * D_MODEL] * _dot(os_ref[...], wus_ref[...])
         + g[:, 2 * D_MODEL:] * _dot(om_ref[...], wum_ref[...]))
    h = x + _dot(m.astype(BF), wo_ref[...])
    h_ref[...] = h
    hn = _rms(h, nf_ref[...]).astype(BF)
    hn_ref[...] = hn

    lt = _dot_nt(wr_ref[...], hn) + br_ref[...]
    grp = [lt[i:i + 1] for i in range(N_GROUPS)]
    gmax = functools.reduce(jnp.maximum, grp)
    gidx = jnp.where(grp[0] == gmax, 0, jnp.where(grp[1] == gmax, 1, jnp.where(grp[2] == gmax, 2, 3)))
    g_gate = 1.0 / functools.reduce(jnp.add, [jnp.exp(v - gmax) for v in grp])
    ins = []
    for k in range(EXPERTS_PER_GROUP):
        v = jnp.zeros((1, tm), F32)
        for gi in range(N_GROUPS):
            r = N_GROUPS + gi * EXPERTS_PER_GROUP + k
            v = jnp.where(gidx == gi, lt[r:r + 1], v)
        ins.append(v)
    imax = functools.reduce(jnp.maximum, ins)
    ex = [jnp.exp(v - imax) for v in ins]
    tot = functools.reduce(jnp.add, ex)
    p = [v / tot for v in ex]

    def top1(vals):
        best = functools.reduce(jnp.maximum, vals)
        idx = jnp.where(vals[0] == best, 0, jnp.where(vals[1] == best, 1, jnp.where(vals[2] == best, 2, 3)))
        return best, idx

    p1, i1 = top1(p)
    p2, i2 = top1([jnp.where(i1 == k, -1.0, p[k]) for k in range(EXPERTS_PER_GROUP)])
    den = p1 + p2
    row = lax.broadcasted_iota(jnp.int32, (8, tm), 0)
    e0 = gidx * EXPERTS_PER_GROUP + i1
    e1 = gidx * EXPERTS_PER_GROUP + i2
    eidx_ref[...] = jnp.where(row == 0, e0, jnp.where(row == 1, e1, 0))
    gate_ref[...] = jnp.where(row == 0, g_gate * p1 / den, jnp.where(row == 1, g_gate * p2 / den, 0.0))


def _merge(x, o_fox, o_sb, o_mem, norm_attn, wg, w_up_fox, w_up_sb, w_up_mem, w_o, norm_ffn, wr, br, tm):
    t = x.shape[0]
    row = lambda w: pl.BlockSpec((tm, w), lambda i: (i, 0))
    col = pl.BlockSpec((8, tm), lambda i: (0, i))
    return pl.pallas_call(
        _merge_body,
        grid=(t // tm,),
        in_specs=[row(D_MODEL), row(FOX_W), row(SB_W), row(MEM_W), _const_spec((1, D_MODEL)),
                  _const_spec((D_MODEL, 3 * D_MODEL)), _const_spec((FOX_W, D_MODEL)), _const_spec((SB_W, D_MODEL)),
                  _const_spec((MEM_W, D_MODEL)), _const_spec((D_MODEL, D_MODEL)), _const_spec((1, D_MODEL)),
                  _const_spec((32, D_MODEL)), _const_spec((32, 1))],
        out_specs=[row(D_MODEL), row(D_MODEL), col, col],
        out_shape=[jax.ShapeDtypeStruct((t, D_MODEL), F32), jax.ShapeDtypeStruct((t, D_MODEL), BF),
                   jax.ShapeDtypeStruct((8, t), jnp.int32), jax.ShapeDtypeStruct((8, t), F32)],
        compiler_params=pltpu.CompilerParams(dimension_semantics=("arbitrary",), vmem_limit_bytes=VMEM_LIMIT),
        name="merge",
    )(x, o_fox, o_sb, o_mem, norm_attn.reshape(1, D_MODEL), wg, w_up_fox, w_up_sb, w_up_mem, w_o,
      norm_ffn.reshape(1, D_MODEL), wr, br)


def _moe_body(cnt_ref, hn_ref, h_ref, irow_ref, icol_ref, gcol_ref, w1_ref, w3_ref, w2_ref, su_ref, sl_ref,
              y_ref, rrow_ref, rc0_ref, rc1_ref, *, tt, r):
    i = pl.program_id(0)
    e = pl.program_id(1)

    @pl.when(e == 0)
    def _():
        y_ref[...] = h_ref[...]
        erow = lax.broadcasted_iota(jnp.int32, (N_EXPERTS, tt), 0)
        m0 = erow == irow_ref[0:1, :]
        m1 = erow == irow_ref[1:2, :]
        ranks = _dot((m0 | m1).astype(BF), su_ref[...])
        rrow_ref[0:1, :] = jnp.sum(jnp.where(m0, ranks, 0.0), axis=0, keepdims=True)
        rrow_ref[1:2, :] = jnp.sum(jnp.where(m1, ranks, 0.0), axis=0, keepdims=True)
        lane = lax.broadcasted_iota(jnp.int32, (tt, LANES), 1)
        c0 = lane == icol_ref[:, 0:1]
        c1 = lane == icol_ref[:, 1:2]
        rcol = _dot(sl_ref[...], (c0 | c1).astype(BF))
        rc0_ref[...] = jnp.sum(jnp.where(c0, rcol, 0.0), axis=1, keepdims=True)
        rc1_ref[...] = jnp.sum(jnp.where(c1, rcol, 0.0), axis=1, keepdims=True)

    n = cnt_ref[i * N_EXPERTS + e]
    sel_row = jnp.where(irow_ref[0:1, :] == e, rrow_ref[0:1, :],
                        jnp.where(irow_ref[1:2, :] == e, rrow_ref[1:2, :], -1.0))
    a0 = icol_ref[:, 0:1] == e
    a1 = icol_ref[:, 1:2] == e
    sel_col = jnp.where(a0, rc0_ref[...], jnp.where(a1, rc1_ref[...], -1.0))
    gate_col = jnp.where(a0, gcol_ref[:, 0:1], jnp.where(a1, gcol_ref[:, 1:2], 0.0))

    def chunk(c, carry):
        base = (c * r).astype(F32)
        p = (sel_row - base == lax.broadcasted_iota(jnp.int32, (r, tt), 0).astype(F32)).astype(BF)
        pt = (sel_col - base == lax.broadcasted_iota(jnp.int32, (tt, r), 1).astype(F32)).astype(BF)
        xc = _dot(p, hn_ref[...]).astype(BF)
        hid = jax.nn.silu(_dot(xc, w1_ref[0])) * _dot(xc, w3_ref[0])
        out = _dot(hid.astype(BF), w2_ref[0])
        y_ref[...] += gate_col * _dot(pt, out.astype(BF))
        return carry

    lax.fori_loop(0, (n + r - 1) // r, chunk, 0)


def _moe(hn, h, eidx, gates, w1, w3, w2, tt, r):
    t = hn.shape[0]
    nt = t // tt
    ids = eidx[:2]
    onehot = (ids[:, :, None] == jnp.arange(N_EXPERTS)[None, None, :]).any(axis=0)
    cnt = onehot.reshape(nt, tt, N_EXPERTS).sum(axis=1).astype(jnp.int32).reshape(-1)
    icol = jnp.transpose(eidx)
    gcol = jnp.transpose(gates)
    pos = np.arange(tt)
    su = jnp.asarray(pos[:, None] < pos[None, :], BF)
    sl = jnp.asarray(pos[:, None] > pos[None, :], BF)
    tok = lambda w: pl.BlockSpec((tt, w), lambda i, e, c: (i, 0))
    wspec = lambda a, b: pl.BlockSpec((1, a, b), lambda i, e, c: (e, 0, 0))
    return pl.pallas_call(
        functools.partial(_moe_body, tt=tt, r=r),
        grid_spec=pltpu.PrefetchScalarGridSpec(
            num_scalar_prefetch=1,
            grid=(nt, N_EXPERTS),
            in_specs=[tok(D_MODEL), tok(D_MODEL), pl.BlockSpec((8, tt), lambda i, e, c: (0, i)), tok(8), tok(8),
                      wspec(D_MODEL, D_EXPERT), wspec(D_MODEL, D_EXPERT), wspec(D_EXPERT, D_MODEL),
                      pl.BlockSpec((tt, tt), lambda i, e, c: (0, 0)), pl.BlockSpec((tt, tt), lambda i, e, c: (0, 0))],
            out_specs=tok(D_MODEL),
            scratch_shapes=[pltpu.VMEM((8, tt), F32), pltpu.VMEM((tt, 1), F32), pltpu.VMEM((tt, 1), F32)]),
        out_shape=jax.ShapeDtypeStruct((t, D_MODEL), F32),
        compiler_params=pltpu.CompilerParams(dimension_semantics=("arbitrary",) * 2,
                                             vmem_limit_bytes=VMEM_LIMIT),
        name="moe",
    )(cnt, hn, h, eidx, icol, gcol, w1, w3, w2, su, sl)


def _block_diag_rows(q_row, width):
    sub = lax.broadcasted_iota(jnp.int32, (8, width), 0)
    lane = lax.broadcasted_iota(jnp.int32, (8, width), 1)
    return jnp.where(lane // HEAD_DIM == sub, jnp.broadcast_to(q_row.astype(F32), (8, width)), 0.0).astype(BF)


def _diag_heads(acc):
    sub = lax.broadcasted_iota(jnp.int32, acc.shape, 0)
    lane = lax.broadcasted_iota(jnp.int32, acc.shape, 1)
    return jnp.sum(jnp.where(lane // HEAD_DIM == sub, acc, 0.0), axis=0, keepdims=True)


def _decode_body(pt_ref, qf_ref, kfn_ref, vfn_ref, lfn_ref, qs_ref, qm_ref, mk_ref, mv_ref, su_ref, *rest, g):
    fk, fv, lf, sk, sv = (rest[i * g:(i + 1) * g] for i in range(5))
    of_ref, os_ref, om_ref, m_ref, l_ref, acc_ref, cf_ref, run_ref, accs_ref = rest[5 * g:]
    j = pl.program_id(1)
    qbd = _block_diag_rows(qf_ref[0], FOX_W)
    qsbd = _block_diag_rows(qs_ref[0], SB_W)

    @pl.when(j == 0)
    def _():
        m_ref[...] = jnp.sum(qbd.astype(F32) * kfn_ref[0].astype(F32), axis=-1, keepdims=True)
        l_ref[...] = jnp.ones_like(l_ref)
        acc_ref[...] = jnp.broadcast_to(vfn_ref[0].astype(F32), acc_ref.shape)
        cf_ref[...] = lfn_ref[0]
        run_ref[...] = jnp.zeros_like(run_ref)
        accs_ref[...] = jnp.zeros_like(accs_ref)

    page = su_ref.shape[0]
    lanes_of = lambda a: jnp.concatenate([r[0].astype(BF) for r in a], axis=1)

    def decay_bias(per_page, carried):
        suf = _suffix_sums(per_page, su_ref[...])
        pieces = []
        for i in range(g):
            rows = slice(8 * i, 8 * i + 8)
            pieces.append(suf[rows] + carried)
            carried = carried + suf[rows, 0:1] + per_page[rows, 0:1]
        return jnp.concatenate(pieces, axis=1), carried

    bias, cf_new = decay_bias(jnp.concatenate([r[0] for r in lf], axis=0), cf_ref[...])
    cf_ref[...] = cf_new
    s = _dot(qbd, lanes_of(fk)) + bias
    m = m_ref[...]
    m_new = jnp.maximum(m, jnp.max(s, axis=-1, keepdims=True))
    p = jnp.exp(s - m_new)
    alpha = jnp.exp(m - m_new)
    l_ref[...] = alpha * l_ref[...] + jnp.sum(p, axis=-1, keepdims=True)
    acc_ref[...] = alpha * acc_ref[...] + _dot_nt(p.astype(BF), lanes_of(fv))
    m_ref[...] = m_new

    z = _dot(qsbd, lanes_of(sk))
    lsn = _log_sigmoid(-z)
    after, run_new = decay_bias(jnp.concatenate([lsn[:, i * page:(i + 1) * page] for i in range(g)], axis=0),
                                run_ref[...])
    run_ref[...] = run_new
    w = jnp.exp(z + lsn + after)
    accs_ref[...] += _dot_nt(w.astype(BF), lanes_of(sv))

    @pl.when(j == pl.num_programs(1) - 1)
    def _():
        of_ref[0] = _diag_heads(acc_ref[...] / l_ref[...]).astype(of_ref.dtype)
        os_ref[0] = _diag_heads(accs_ref[...]).astype(os_ref.dtype)
        s = _dot(_block_diag_rows(qm_ref[0], MEM_W), mk_ref[0].astype(BF))
        e = jnp.exp(s - jnp.max(s, axis=-1, keepdims=True))
        p = e / jnp.sum(e, axis=-1, keepdims=True)
        om_ref[0] = _diag_heads(_dot_nt(p.astype(BF), mv_ref[0].astype(BF))).astype(om_ref.dtype)


def _decode_attend(page_table, qf, kfn, vfn, lfn, qs, qm, fk, fv, lfc, sk, sv, mk, mv, g):
    nb, npg = page_table.shape
    page = fk.shape[2]
    per_sample = lambda a, b: pl.BlockSpec((1, a, b), lambda i, j, pt: (i, 0, 0))

    def paged(rows, gi):
        return pl.BlockSpec((1, rows, page), lambda i, j, pt: (pt[i * npg + npg - 1 - (j * g + gi)], 0, 0))

    pos = np.arange(page)
    su = jnp.asarray(pos[:, None] > pos[None, :], BF)
    in_specs = [per_sample(1, FOX_W), per_sample(1, FOX_W), per_sample(1, FOX_W), per_sample(8, 1),
                per_sample(1, SB_W), per_sample(1, MEM_W), per_sample(MEM_W, mk.shape[2]),
                per_sample(MEM_W, mk.shape[2]), pl.BlockSpec((page, page), lambda i, j, pt: (0, 0))]
    operands = [qf, kfn, vfn, lfn, qs, qm, mk, mv, su]
    for arr, rows in ((fk, FOX_W), (fv, FOX_W), (lfc, H_FOX), (sk, SB_W), (sv, SB_W)):
        in_specs += [paged(rows, gi) for gi in range(g)]
        operands += [arr] * g
    return pl.pallas_call(
        functools.partial(_decode_body, g=g),
        grid_spec=pltpu.PrefetchScalarGridSpec(
            num_scalar_prefetch=1,
            grid=(nb, npg // g),
            in_specs=in_specs,
            out_specs=[per_sample(1, FOX_W), per_sample(1, SB_W), per_sample(1, MEM_W)],
            scratch_shapes=[pltpu.VMEM((8, 1), F32), pltpu.VMEM((8, 1), F32), pltpu.VMEM((8, FOX_W), F32),
                            pltpu.VMEM((8, 1), F32), pltpu.VMEM((8, 1), F32), pltpu.VMEM((8, SB_W), F32)]),
        out_shape=[jax.ShapeDtypeStruct((nb, 1, w), BF) for w in (FOX_W, SB_W, MEM_W)],
        compiler_params=pltpu.CompilerParams(dimension_semantics=("arbitrary",) * 2,
                                             vmem_limit_bytes=VMEM_LIMIT),
        name="decode_attend",
    )(page_table.reshape(-1), *operands)


def _router_params(w_grp, b_grp, w_exp, b_exp):
    n = N_GROUPS + N_EXPERTS
    wr = jnp.zeros((32, D_MODEL), BF).at[:n].set(jnp.concatenate([w_grp, w_exp], axis=1).T.astype(BF))
    br = jnp.zeros((32, 1), F32).at[:n, 0].set(jnp.concatenate([b_grp, b_exp]))
    return wr, br


def _split_w_in(w_in):
    o1 = 3 * FOX_W
    o2 = o1 + H_FOX
    o3 = o2 + 3 * SB_W + MEM_W
    wa = jnp.concatenate([w_in[:, :o1], w_in[:, o2:o3]], axis=1).astype(BF)
    wf = jnp.zeros((D_MODEL, LANES), BF).at[:, :H_FOX].set(w_in[:, o1:o2].astype(BF))
    return wa, wf, w_in[:, o3:].astype(BF)


def _position_minor(cache):
    pool, page, h, dh = cache.shape
    return jnp.transpose(cache, (0, 2, 3, 1)).reshape(pool, h * dh, page)


def kernel(x_prompt, x_sample, mem_prompt, cache_fox_k, cache_fox_v, cache_fox_logf, cache_sb_k, cache_sb_v,
           cache_mem_k, cache_mem_v, page_table, norm_attn, w_in, b_forget, g_fox_q, g_fox_k, g_mem_q, g_mem_k,
           norm_mem, w_mem_k, w_mem_v, w_up_fox, w_up_sb, w_up_mem, w_o, norm_ffn, w_grp, b_grp, w_exp, b_exp,
           w1, w3, w2):
    assert w_in.shape[0] == 1, "one layer"
    b, s, _ = x_prompt.shape
    nb = x_sample.shape[0]
    wa, wf, wg = _split_w_in(w_in[0])
    wr, br = _router_params(w_grp[0], b_grp[0], w_exp[0], b_exp[0])
    merge_w = (norm_attn[0], wg, w_up_fox[0].astype(BF), w_up_sb[0].astype(BF), w_up_mem[0].astype(BF),
               w_o[0].astype(BF), norm_ffn[0], wr, br)
    moe_w = (w1[0].astype(BF), w3[0].astype(BF), w2[0].astype(BF))
    proj_w = (norm_attn[0], wa, wf, b_forget[0], g_fox_q[0], g_fox_k[0], g_mem_q[0])

    qf, kf, kfb, vf, vfb, lf, cexp, qs, ks, ksb, vs, vsb, qm = _inproj(x_prompt, *proj_w, tm=PROMPT_TM)
    ck = jnp.swapaxes(cexp[..., ::HEAD_DIM], 1, 2).reshape(b, H_FOX, 1, s)
    o_f = _fox_prompt(qf, kfb, vfb, cexp, ck, tq=ATTN_T, tk=FOX_TK)
    o_s = _sb_prompt(qs, ksb, vsb, t=ATTN_T)
    mk, mkb, mv, mvb = _mem_kv(mem_prompt, norm_mem[0], w_mem_k[0], w_mem_v[0], g_mem_k[0])
    o_m = _mem_attend(qm, mkb, mvb, tm=PROMPT_TM)
    t = b * s
    flat = lambda a: a.reshape(t, a.shape[-1])
    h, hn, eidx, gates = _merge(flat(x_prompt), flat(o_f), flat(o_s), flat(o_m), *merge_w, tm=PROMPT_TM)
    y_prompt = _moe(hn, h, eidx, gates, *moe_w, tt=MOE_TT, r=MOE_R).reshape(b, s, D_MODEL)

    sq = _inproj(x_sample.reshape(1, nb, D_MODEL), *proj_w, tm=nb)
    qf2, kf2, kfb2, vf2, vfb2, lf2, _, qs2, ks2, _, vs2, _, qm2 = sq
    per = lambda a: a.reshape(nb, 1, a.shape[-1])
    o_f2, o_s2, o_m2 = _decode_attend(
        page_table, per(qf2), per(kfb2), per(vfb2), lf2[0, :, :H_FOX].reshape(nb, H_FOX, 1), per(qs2), per(qm2),
        _position_minor(cache_fox_k[0]), _position_minor(cache_fox_v[0]),
        jnp.transpose(cache_fox_logf[0], (0, 2, 1)), _position_minor(cache_sb_k[0]), _position_minor(cache_sb_v[0]),
        _position_minor(cache_mem_k[0]), _position_minor(cache_mem_v[0]), g=DECODE_PAGES_PER_STEP)
    flat2 = lambda a: a.reshape(nb, a.shape[-1])
    h2, hn2, eidx2, gates2 = _merge(flat2(x_sample), flat2(o_f2), flat2(o_s2), flat2(o_m2), *merge_w, tm=nb)
    y_sample = _moe(hn2, h2, eidx2, gates2, *moe_w, tt=nb, r=nb).reshape(nb, 1, D_MODEL)

    heads = lambda a, n: a.reshape(1, a.shape[0], a.shape[1], n, HEAD_DIM)
    dec = lambda a, n: a.reshape(1, nb, 1, n, HEAD_DIM)
    return (y_prompt, y_sample,
            heads(kf, H_FOX), heads(vf, H_FOX), lf[..., :H_FOX].reshape(1, b, s, H_FOX),
            heads(ks, H_SB), heads(vs, H_SB), heads(mk, H_MEM), heads(mv, H_MEM),
            dec(kf2, H_FOX), dec(vf2, H_FOX), lf2[..., :H_FOX].reshape(1, nb, 1, H_FOX),
            dec(ks2, H_SB), dec(vs2, H_SB))
```

```python
import functools

import numpy as np
import jax
import jax.numpy as jnp
from jax import lax
from jax.experimental import pallas as pl
from jax.experimental.pallas import tpu as pltpu

BF = jnp.bfloat16
F32 = jnp.float32

D_MODEL = 1024
HEAD_DIM = 64
H_FOX = 8
H_SB = 4
H_MEM = 4
FOX_W = H_FOX * HEAD_DIM
SB_W = H_SB * HEAD_DIM
MEM_W = H_MEM * HEAD_DIM
N_GROUPS = 4
EXPERTS_PER_GROUP = 4
N_EXPERTS = N_GROUPS * EXPERTS_PER_GROUP
D_EXPERT = 512
EPS = 1e-6
ATTN_SCALE = HEAD_DIM ** -0.5
LANES = 128
A_W = 3 * FOX_W + 3 * SB_W + MEM_W
VMEM_LIMIT = 56 * 1024 * 1024
PROMPT_TM = 256
ATTN_T = 256
MOE_TT = 1024
FOX_TK = 512
SB_TK = 512
MOE_R = 256
DECODE_PAGES_PER_STEP = 16


def _dot(a, b):
    return jnp.dot(a, b, preferred_element_type=F32)


def _dot_nt(a, b):
    return lax.dot_general(a, b, (((1,), (1,)), ((), ())), preferred_element_type=F32)


def _split3(a):
    a1 = a.astype(BF)
    r = a - a1.astype(F32)
    a2 = r.astype(BF)
    a3 = (r - a2.astype(F32)).astype(BF)
    return a1, a2, a3


def _exact_left(m, b):
    b1, b2, b3 = _split3(b)
    return _dot(m, b1) + _dot(m, b2) + _dot(m, b3)


def _exact_right(a, m):
    a1, a2, a3 = _split3(a)
    return _dot(a1, m) + _dot(a2, m) + _dot(a3, m)


def _suffix_sums(a, later):
    hi = a.astype(BF)
    lo = (a - hi.astype(F32)).astype(BF)
    return _dot(hi, later) + _dot(lo, later)


def _log_sigmoid(x):
    return jnp.minimum(x, 0.0) - jnp.log1p(jnp.exp(-jnp.abs(x)))


def _rms(x, g):
    return x * lax.rsqrt(jnp.mean(x * x, axis=-1, keepdims=True) + EPS) * g


def _head_norm(t, head_mean, g):
    ms = _dot((t * t).astype(BF), head_mean)
    return t * lax.rsqrt(ms + EPS) * g


def _const_spec(shape):
    return pl.BlockSpec(shape, lambda *_: (0,) * len(shape))


def _head_mean_matrix(width):
    i = np.arange(width)
    return jnp.asarray((i[:, None] // HEAD_DIM == i[None, :] // HEAD_DIM) / HEAD_DIM, BF)


def _inproj_body(x_ref, na_ref, wa_ref, wf_ref, bfg_ref, gfq_ref, gfk_ref, gmq_ref, hm512_ref, hm256_ref,
                 tri_ref, exp_ref,
                 qf_ref, kf_ref, kfb_ref, vf_ref, vfb_ref, lf_ref, cexp_ref, qs_ref, ks_ref, ksb_ref,
                 vs_ref, vsb_ref, qm_ref, carry_ref):
    tm = x_ref.shape[1]

    @pl.when(pl.program_id(1) == 0)
    def _():
        carry_ref[...] = jnp.zeros_like(carry_ref)

    xn = _rms(x_ref[0], na_ref[...]).astype(BF)
    y = _dot(xn, wa_ref[...])
    f = _dot(xn, wf_ref[...])

    qf = _head_norm(y[:, 0:FOX_W], hm512_ref[...], gfq_ref[...])
    qf_ref[0] = (qf * ATTN_SCALE).astype(BF)
    kf = _head_norm(y[:, FOX_W:2 * FOX_W], hm512_ref[...], gfk_ref[...])
    kf_ref[0] = kf
    kfb_ref[0] = kf.astype(BF)
    vf = y[:, 2 * FOX_W:3 * FOX_W]
    vf_ref[0] = vf
    vfb_ref[0] = vf.astype(BF)
    o = 3 * FOX_W
    qs_ref[0] = (y[:, o:o + SB_W] * ATTN_SCALE).astype(BF)
    ks = y[:, o + SB_W:o + 2 * SB_W]
    ks_ref[0] = ks
    ksb_ref[0] = ks.astype(BF)
    vs = y[:, o + 2 * SB_W:o + 3 * SB_W]
    vs_ref[0] = vs
    vsb_ref[0] = vs.astype(BF)
    qm = _head_norm(y[:, o + 3 * SB_W:o + 3 * SB_W + MEM_W], hm256_ref[...], gmq_ref[...])
    qm_ref[0] = (qm * ATTN_SCALE).astype(BF)

    lane = lax.broadcasted_iota(jnp.int32, (tm, LANES), 1)
    lf = jnp.where(lane < H_FOX, _log_sigmoid(f + bfg_ref[...]), 0.0)
    lf_ref[0] = lf
    c = carry_ref[...] + _exact_left(tri_ref[...], lf)
    carry_ref[...] = c[tm - 1:tm, :]
    cexp_ref[0] = _exact_right(c, exp_ref[...])


def _inproj(x, norm_attn, wa, wf, b_forget, g_fox_q, g_fox_k, g_mem_q, tm):
    b, s, _ = x.shape
    nt = s // tm
    tile = lambda w: pl.BlockSpec((1, tm, w), lambda i, j: (i, j, 0))
    rep = lambda g, n: jnp.tile(g.reshape(1, HEAD_DIM), (1, n))
    bfg = jnp.zeros((1, LANES), F32).at[0, :H_FOX].set(b_forget)
    tri = jnp.asarray(np.tril(np.ones((tm, tm))), BF)
    i = np.arange(FOX_W)
    expand = jnp.asarray(np.arange(LANES)[:, None] == i[None, :] // HEAD_DIM, BF)
    shapes = [(FOX_W, BF), (FOX_W, F32), (FOX_W, BF), (FOX_W, F32), (FOX_W, BF), (LANES, F32), (FOX_W, F32),
              (SB_W, BF), (SB_W, F32), (SB_W, BF), (SB_W, F32), (SB_W, BF), (MEM_W, BF)]
    return pl.pallas_call(
        _inproj_body,
        grid=(b, nt),
        in_specs=[tile(D_MODEL), _const_spec((1, D_MODEL)), _const_spec((D_MODEL, A_W)),
                  _const_spec((D_MODEL, LANES)), _const_spec((1, LANES)), _const_spec((1, FOX_W)),
                  _const_spec((1, FOX_W)), _const_spec((1, MEM_W)), _const_spec((FOX_W, FOX_W)),
                  _const_spec((MEM_W, MEM_W)), _const_spec((tm, tm)), _const_spec((LANES, FOX_W))],
        out_specs=[tile(w) for w, _ in shapes],
        out_shape=[jax.ShapeDtypeStruct((b, s, w), d) for w, d in shapes],
        scratch_shapes=[pltpu.VMEM((1, LANES), F32)],
        compiler_params=pltpu.CompilerParams(dimension_semantics=("arbitrary", "arbitrary"),
                                             vmem_limit_bytes=VMEM_LIMIT),
        name="inproj",
    )(x, norm_attn.reshape(1, D_MODEL), wa, wf, bfg, rep(g_fox_q, H_FOX), rep(g_fox_k, H_FOX),
      rep(g_mem_q, H_MEM), _head_mean_matrix(FOX_W), _head_mean_matrix(MEM_W), tri, expand)


def _head_masked(q, n_heads):
    t = q.shape[0]
    lane = lax.broadcasted_iota(jnp.int32, (t, LANES), 1)
    out = []
    for h in range(n_heads):
        pair = q[:, (h // 2) * LANES:(h // 2 + 1) * LANES]
        out.append(jnp.where(lane // HEAD_DIM == h % 2, pair, 0.0).astype(BF))
    return out


def _merge_pairs(per_head):
    t = per_head[0].shape[0]
    lane = lax.broadcasted_iota(jnp.int32, (t, LANES), 1)
    pairs = [jnp.where(lane < HEAD_DIM, per_head[h], per_head[h + 1]) for h in range(0, len(per_head), 2)]
    return jnp.concatenate(pairs, axis=1)


def _fox_body(q_ref, k_ref, v_ref, cq_ref, ck_ref, o_ref, *, tq, tk):
    qi = pl.program_id(1)
    row = lax.broadcasted_iota(jnp.int32, (tq, tk), 0)
    col = lax.broadcasted_iota(jnp.int32, (tq, tk), 1)
    qh = _head_masked(q_ref[0].astype(F32), H_FOX)
    cq = cq_ref[0]
    cqh = [jnp.broadcast_to(cq[:, h * HEAD_DIM:h * HEAD_DIM + 1], (tq, tk)) for h in range(H_FOX)]

    def step(kt, carry, masked):
        k0 = pl.multiple_of(kt * tk, tk)
        new = []
        for h in range(H_FOX):
            lanes = pl.ds((h // 2) * LANES, LANES)
            m, l, acc = carry[3 * h:3 * h + 3]
            s = _dot_nt(qh[h], k_ref[0, pl.ds(k0, tk), lanes]) + (cqh[h] - ck_ref[0, h, :, pl.ds(k0, tk)])
            if masked:
                s = jnp.where(k0 + col <= qi * tq + row, s, -jnp.inf)
            m_new = jnp.maximum(m, jnp.max(s, axis=-1, keepdims=True))
            p = jnp.exp(s - m_new)
            alpha = jnp.exp(m - m_new)
            l = alpha * l + jnp.sum(p, axis=-1, keepdims=True)
            acc = alpha * acc + _dot(p.astype(BF), v_ref[0, pl.ds(k0, tk), lanes])
            new += [m_new, l, acc]
        return tuple(new)

    init = (jnp.full((tq, 1), -jnp.inf, F32), jnp.zeros((tq, 1), F32), jnp.zeros((tq, LANES), F32)) * H_FOX
    n_full = (qi * tq) // tk
    carry = lax.fori_loop(0, n_full, lambda kt, c: step(kt, c, False), init)
    fin = step(n_full, carry, True)
    o_ref[0] = _merge_pairs([fin[3 * h + 2] / fin[3 * h + 1] for h in range(H_FOX)]).astype(o_ref.dtype)


def _fox_prompt(qf, kfb, vfb, cexp, ck, tq, tk):
    b, s, _ = qf.shape
    assert tk % tq == 0 and s % tk == 0
    qspec = pl.BlockSpec((1, tq, FOX_W), lambda i, j: (i, j, 0))
    kspec = pl.BlockSpec((1, s, FOX_W), lambda i, j: (i, 0, 0))
    return pl.pallas_call(
        functools.partial(_fox_body, tq=tq, tk=tk),
        grid=(b, s // tq),
        in_specs=[qspec, kspec, kspec, qspec, pl.BlockSpec((1, H_FOX, 1, s), lambda i, j: (i, 0, 0, 0))],
        out_specs=qspec,
        out_shape=jax.ShapeDtypeStruct((b, s, FOX_W), BF),
        compiler_params=pltpu.CompilerParams(dimension_semantics=("arbitrary",) * 2,
                                             vmem_limit_bytes=VMEM_LIMIT),
        name="fox_prompt",
    )(qf, kfb, vfb, cexp, ck)


def _sb_body(q_ref, k_ref, v_ref, u_ref, o_ref, *, tq, tk):
    qi = pl.program_id(1)
    row = lax.broadcasted_iota(jnp.int32, (tq, tk), 0)
    col = lax.broadcasted_iota(jnp.int32, (tq, tk), 1)
    qh = _head_masked(q_ref[0].astype(F32), H_SB)
    nblk = tk // LANES

    def suffix_sums(lm, later):
        stacked = jnp.concatenate([lm[:, b * LANES:(b + 1) * LANES] for b in range(nblk)], axis=0)
        hi = stacked.astype(BF)
        lo = (stacked - hi.astype(F32)).astype(BF)
        loc = _dot(hi, u_ref[...]) + _dot(lo, u_ref[...])
        out = [None] * nblk
        for b in reversed(range(nblk)):
            lb = loc[b * tq:(b + 1) * tq]
            out[b] = lb + later
            later = later + lb[:, 0:1]
        return jnp.concatenate(out, axis=1), later

    def step(kt, carry, diag):
        k0 = pl.multiple_of(kt * tk, tk)
        valid = k0 + col < qi * tq + row
        new = []
        for h in range(H_SB):
            lanes = pl.ds((h // 2) * LANES, LANES)
            run, acc = carry[2 * h:2 * h + 2]
            z = _dot_nt(qh[h], k_ref[0, pl.ds(k0, tk), lanes])
            lsn = jnp.minimum(-z, 0.0) - jnp.log(1.0 + jnp.exp(-jnp.abs(z)))
            incl, run = suffix_sums(jnp.where(valid, lsn, 0.0) if diag else lsn, run)
            w = jnp.exp(z + incl)
            if diag:
                w = jnp.where(valid, w, 0.0)
            new += [run, acc + _dot(w.astype(BF), v_ref[0, pl.ds(k0, tk), lanes])]
        return tuple(new)

    init = (jnp.zeros((tq, 1), F32), jnp.zeros((tq, LANES), F32)) * H_SB
    n_full = (qi * tq) // tk
    carry = step(n_full, init, True)
    fin = lax.fori_loop(0, n_full, lambda i, c: step(n_full - 1 - i, c, False), carry)
    o_ref[0] = _merge_pairs([fin[2 * h + 1] for h in range(H_SB)]).astype(o_ref.dtype)


def _sb_prompt(qs, ksb, vsb, tq, tk):
    b, s, _ = qs.shape
    assert tk % tq == 0 and s % tk == 0
    qspec = pl.BlockSpec((1, tq, SB_W), lambda i, j: (i, j, 0))
    kspec = pl.BlockSpec((1, s, SB_W), lambda i, j: (i, 0, 0))
    u = jnp.asarray(np.arange(LANES)[:, None] >= np.arange(LANES)[None, :], BF)
    return pl.pallas_call(
        functools.partial(_sb_body, tq=tq, tk=tk),
        grid=(b, s // tq),
        in_specs=[qspec, kspec, kspec, _const_spec((LANES, LANES))],
        out_specs=qspec,
        out_shape=jax.ShapeDtypeStruct((b, s, SB_W), BF),
        compiler_params=pltpu.CompilerParams(dimension_semantics=("arbitrary",) * 2,
                                             vmem_limit_bytes=VMEM_LIMIT),
        name="sb_prompt",
    )(qs, ksb, vsb, u)


def _memkv_body(mem_ref, nm_ref, wk_ref, wv_ref, gk_ref, hm_ref, mk_ref, mkb_ref, mv_ref, mvb_ref):
    mn = _rms(mem_ref[0], nm_ref[...]).astype(BF)
    mk = _head_norm(_dot(mn, wk_ref[...]), hm_ref[...], gk_ref[...])
    mv = _dot(mn, wv_ref[...])
    mk_ref[0] = mk
    mkb_ref[0] = mk.astype(BF)
    mv_ref[0] = mv
    mvb_ref[0] = mv.astype(BF)


def _mem_kv(mem, norm_mem, w_mem_k, w_mem_v, g_mem_k):
    b, n, _ = mem.shape
    spec = pl.BlockSpec((1, n, MEM_W), lambda i: (i, 0, 0))
    return pl.pallas_call(
        _memkv_body,
        grid=(b,),
        in_specs=[pl.BlockSpec((1, n, D_MODEL), lambda i: (i, 0, 0)), _const_spec((1, D_MODEL)),
                  _const_spec((D_MODEL, MEM_W)), _const_spec((D_MODEL, MEM_W)), _const_spec((1, MEM_W)),
                  _const_spec((MEM_W, MEM_W))],
        out_specs=[spec] * 4,
        out_shape=[jax.ShapeDtypeStruct((b, n, MEM_W), d) for d in (F32, BF, F32, BF)],
        compiler_params=pltpu.CompilerParams(dimension_semantics=("arbitrary",), vmem_limit_bytes=VMEM_LIMIT),
        name="mem_kv",
    )(mem, norm_mem.reshape(1, D_MODEL), w_mem_k.astype(BF), w_mem_v.astype(BF),
      jnp.tile(g_mem_k.reshape(1, HEAD_DIM), (1, H_MEM)), _head_mean_matrix(MEM_W))


def _memattn_body(q_ref, mk_ref, mv_ref, o_ref):
    q = q_ref[0].astype(F32)
    lane = lax.broadcasted_iota(jnp.int32, q.shape, 1)
    out = jnp.zeros(q.shape, F32)
    for h in range(H_MEM):
        qh = jnp.where(lane // HEAD_DIM == h, q, 0.0).astype(BF)
        s = _dot_nt(qh, mk_ref[0])
        e = jnp.exp(s - jnp.max(s, axis=-1, keepdims=True))
        p = e / jnp.sum(e, axis=-1, keepdims=True)
        out = jnp.where(lane // HEAD_DIM == h, _dot(p.astype(BF), mv_ref[0]), out)
    o_ref[0] = out.astype(o_ref.dtype)


def _mem_attend(qm, mkb, mvb, tm):
    b, s, _ = qm.shape
    n = mkb.shape[1]
    qspec = pl.BlockSpec((1, tm, MEM_W), lambda i, j: (i, j, 0))
    kspec = pl.BlockSpec((1, n, MEM_W), lambda i, j: (i, 0, 0))
    return pl.pallas_call(
        _memattn_body,
        grid=(b, s // tm),
        in_specs=[qspec, kspec, kspec],
        out_specs=qspec,
        out_shape=jax.ShapeDtypeStruct((b, s, MEM_W), BF),
        compiler_params=pltpu.CompilerParams(dimension_semantics=("arbitrary",) * 2,
                                             vmem_limit_bytes=VMEM_LIMIT),
        name="mem_attend",
    )(qm, mkb, mvb)


def _merge_body(x_ref, of_ref, os_ref, om_ref, na_ref, wg_ref, wuf_ref, wus_ref, wum_ref, wo_ref, nf_ref,
                wr_ref, br_ref, h_ref, hn_ref, eidx_ref, gate_ref):
    x = x_ref[...]
    tm = x.shape[0]
    xn = _rms(x, na_ref[...]).astype(BF)
    g = jax.nn.sigmoid(_dot(xn, wg_ref[...]))
    m = (g[:, :D_MODEL] * _dot(of_ref[...], wuf_ref[...])
         + g[:, D_MODEL:2 * D_MODEL] * _dot(os_ref[...], wus_ref[...])
         + g[:, 2 * D_MODEL:] * _dot(om_ref[...], wum_ref[...]))
    h = x + _dot(m.astype(BF), wo_ref[...])
    h_ref[...] = h
    hn = _rms(h, nf_ref[...]).astype(BF)
    hn_ref[...] = hn

    lt = _dot_nt(wr_ref[...], hn) + br_ref[...]
    grp = [lt[i:i + 1] for i in range(N_GROUPS)]
    gmax = functools.reduce(jnp.maximum, grp)
    gidx = jnp.where(grp[0] == gmax, 0, jnp.where(grp[1] == gmax, 1, jnp.where(grp[2] == gmax, 2, 3)))
    g_gate = 1.0 / functools.reduce(jnp.add, [jnp.exp(v - gmax) for v in grp])
    ins = []
    for k in range(EXPERTS_PER_GROUP):
        v = jnp.zeros((1, tm), F32)
        for gi in range(N_GROUPS):
            r = N_GROUPS + gi * EXPERTS_PER_GROUP + k
            v = jnp.where(gidx == gi, lt[r:r + 1], v)
        ins.append(v)
    imax = functools.reduce(jnp.maximum, ins)
    ex = [jnp.exp(v - imax) for v in ins]
    tot = functools.reduce(jnp.add, ex)
    p = [v / tot for v in ex]

    def top1(vals):
        best = functools.reduce(jnp.maximum, vals)
        idx = jnp.where(vals[0] == best, 0, jnp.where(vals[1] == best, 1, jnp.where(vals[2] == best, 2, 3)))
        return best, idx

    p1, i1 = top1(p)
    p2, i2 = top1([jnp.where(i1 == k, -1.0, p[k]) for k in range(EXPERTS_PER_GROUP)])
    den = p1 + p2
    row = lax.broadcasted_iota(jnp.int32, (8, tm), 0)
    e0 = gidx * EXPERTS_PER_GROUP + i1
    e1 = gidx * EXPERTS_PER_GROUP + i2
    eidx_ref[...] = jnp.where(row == 0, e0, jnp.where(row == 1, e1, 0))
    gate_ref[...] = jnp.where(row == 0, g_gate * p1 / den, jnp.where(row == 1, g_gate * p2 / den, 0.0))


def _merge(x, o_fox, o_sb, o_mem, norm_attn, wg, w_up_fox, w_up_sb, w_up_mem, w_o, norm_ffn, wr, br, tm):
    t = x.shape[0]
    row = lambda w: pl.BlockSpec((tm, w), lambda i: (i, 0))
    col = pl.BlockSpec((8, tm), lambda i: (0, i))
    return pl.pallas_call(
        _merge_body,
        grid=(t // tm,),
        in_specs=[row(D_MODEL), row(FOX_W), row(SB_W), row(MEM_W), _const_spec((1, D_MODEL)),
                  _const_spec((D_MODEL, 3 * D_MODEL)), _const_spec((FOX_W, D_MODEL)), _const_spec((SB_W, D_MODEL)),
                  _const_spec((MEM_W, D_MODEL)), _const_spec((D_MODEL, D_MODEL)), _const_spec((1, D_MODEL)),
                  _const_spec((32, D_MODEL)), _const_spec((32, 1))],
        out_specs=[row(D_MODEL), row(D_MODEL), col, col],
        out_shape=[jax.ShapeDtypeStruct((t, D_MODEL), F32), jax.ShapeDtypeStruct((t, D_MODEL), BF),
                   jax.ShapeDtypeStruct((8, t), jnp.int32), jax.ShapeDtypeStruct((8, t), F32)],
        compiler_params=pltpu.CompilerParams(dimension_semantics=("arbitrary",), vmem_limit_bytes=VMEM_LIMIT),
        name="merge",
    )(x, o_fox, o_sb, o_mem, norm_attn.reshape(1, D_MODEL), wg, w_up_fox, w_up_sb, w_up_mem, w_o,
      norm_ffn.reshape(1, D_MODEL), wr, br)


def _moe_body(cnt_ref, hn_ref, h_ref, irow_ref, icol_ref, gcol_ref, w1_ref, w3_ref, w2_ref, su_ref, sl_ref,
              y_ref, rrow_ref, rc0_ref, rc1_ref, *, tt, r):
    i = pl.program_id(0)
    e = pl.program_id(1)

    @pl.when(e == 0)
    def _():
        y_ref[...] = h_ref[...]
        erow = lax.broadcasted_iota(jnp.int32, (N_EXPERTS, tt), 0)
        m0 = erow == irow_ref[0:1, :]
        m1 = erow == irow_ref[1:2, :]
        ranks = _dot((m0 | m1).astype(BF), su_ref[...])
        rrow_ref[0:1, :] = jnp.sum(jnp.where(m0, ranks, 0.0), axis=0, keepdims=True)
        rrow_ref[1:2, :] = jnp.sum(jnp.where(m1, ranks, 0.0), axis=0, keepdims=True)
        lane = lax.broadcasted_iota(jnp.int32, (tt, LANES), 1)
        c0 = lane == icol_ref[:, 0:1]
        c1 = lane == icol_ref[:, 1:2]
        rcol = _dot(sl_ref[...], (c0 | c1).astype(BF))
        rc0_ref[...] = jnp.sum(jnp.where(c0, rcol, 0.0), axis=1, keepdims=True)
        rc1_ref[...] = jnp.sum(jnp.where(c1, rcol, 0.0), axis=1, keepdims=True)

    n = cnt_ref[i * N_EXPERTS + e]
    sel_row = jnp.where(irow_ref[0:1, :] == e, rrow_ref[0:1, :],
                        jnp.where(irow_ref[1:2, :] == e, rrow_ref[1:2, :], -1.0))
    a0 = icol_ref[:, 0:1] == e
    a1 = icol_ref[:, 1:2] == e
    sel_col = jnp.where(a0, rc0_ref[...], jnp.where(a1, rc1_ref[...], -1.0))
    gate_col = jnp.where(a0, gcol_ref[:, 0:1], jnp.where(a1, gcol_ref[:, 1:2], 0.0))

    def chunk(c, carry):
        base = (c * r).astype(F32)
        p = (sel_row - base == lax.broadcasted_iota(jnp.int32, (r, tt), 0).astype(F32)).astype(BF)
        pt = (sel_col - base == lax.broadcasted_iota(jnp.int32, (tt, r), 1).astype(F32)).astype(BF)
        xc = _dot(p, hn_ref[...]).astype(BF)
        hid = jax.nn.silu(_dot(xc, w1_ref[0])) * _dot(xc, w3_ref[0])
        out = _dot(hid.astype(BF), w2_ref[0])
        y_ref[...] += gate_col * _dot(pt, out.astype(BF))
        return carry

    lax.fori_loop(0, (n + r - 1) // r, chunk, 0)


def _moe(hn, h, eidx, gates, w1, w3, w2, tt, r):
    t = hn.shape[0]
    nt = t // tt
    ids = eidx[:2]
    onehot = (ids[:, :, None] == jnp.arange(N_EXPERTS)[None, None, :]).any(axis=0)
    cnt = onehot.reshape(nt, tt, N_EXPERTS).sum(axis=1).astype(jnp.int32).reshape(-1)
    icol = jnp.transpose(eidx)
    gcol = jnp.transpose(gates)
    pos = np.arange(tt)
    su = jnp.asarray(pos[:, None] < pos[None, :], BF)
    sl = jnp.asarray(pos[:, None] > pos[None, :], BF)
    tok = lambda w: pl.BlockSpec((tt, w), lambda i, e, c: (i, 0))
    wspec = lambda a, b: pl.BlockSpec((1, a, b), lambda i, e, c: (e, 0, 0))
    return pl.pallas_call(
        functools.partial(_moe_body, tt=tt, r=r),
        grid_spec=pltpu.PrefetchScalarGridSpec(
            num_scalar_prefetch=1,
            grid=(nt, N_EXPERTS),
            in_specs=[tok(D_MODEL), tok(D_MODEL), pl.BlockSpec((8, tt), lambda i, e, c: (0, i)), tok(8), tok(8),
                      wspec(D_MODEL, D_EXPERT), wspec(D_MODEL, D_EXPERT), wspec(D_EXPERT, D_MODEL),
                      pl.BlockSpec((tt, tt), lambda i, e, c: (0, 0)), pl.BlockSpec((tt, tt), lambda i, e, c: (0, 0))],
            out_specs=tok(D_MODEL),
            scratch_shapes=[pltpu.VMEM((8, tt), F32), pltpu.VMEM((tt, 1), F32), pltpu.VMEM((tt, 1), F32)]),
        out_shape=jax.ShapeDtypeStruct((t, D_MODEL), F32),
        compiler_params=pltpu.CompilerParams(dimension_semantics=("arbitrary",) * 2,
                                             vmem_limit_bytes=VMEM_LIMIT),
        name="moe",
    )(cnt, hn, h, eidx, icol, gcol, w1, w3, w2, su, sl)


def _block_diag_rows(q_row, width):
    sub = lax.broadcasted_iota(jnp.int32, (8, width), 0)
    lane = lax.broadcasted_iota(jnp.int32, (8, width), 1)
    return jnp.where(lane // HEAD_DIM == sub, jnp.broadcast_to(q_row.astype(F32), (8, width)), 0.0).astype(BF)


def _diag_heads(acc):
    sub = lax.broadcasted_iota(jnp.int32, acc.shape, 0)
    lane = lax.broadcasted_iota(jnp.int32, acc.shape, 1)
    return jnp.sum(jnp.where(lane // HEAD_DIM == sub, acc, 0.0), axis=0, keepdims=True)


def _decode_body(pt_ref, qf_ref, kfn_ref, vfn_ref, lfn_ref, qs_ref, qm_ref, mk_ref, mv_ref, su_ref, *rest, g):
    fk, fv, lf, sk, sv = (rest[i * g:(i + 1) * g] for i in range(5))
    of_ref, os_ref, om_ref, m_ref, l_ref, acc_ref, cf_ref, run_ref, accs_ref = rest[5 * g:]
    j = pl.program_id(1)
    qbd = _block_diag_rows(qf_ref[0], FOX_W)
    qsbd = _block_diag_rows(qs_ref[0], SB_W)

    @pl.when(j == 0)
    def _():
        m_ref[...] = jnp.sum(qbd.astype(F32) * kfn_ref[0].astype(F32), axis=-1, keepdims=True)
        l_ref[...] = jnp.ones_like(l_ref)
        acc_ref[...] = jnp.broadcast_to(vfn_ref[0].astype(F32), acc_ref.shape)
        cf_ref[...] = lfn_ref[0]
        run_ref[...] = jnp.zeros_like(run_ref)
        accs_ref[...] = jnp.zeros_like(accs_ref)

    page = su_ref.shape[0]
    lanes_of = lambda a: jnp.concatenate([r[0].astype(BF) for r in a], axis=1)

    def decay_bias(per_page, carried):
        suf = _suffix_sums(per_page, su_ref[...])
        pieces = []
        for i in range(g):
            rows = slice(8 * i, 8 * i + 8)
            pieces.append(suf[rows] + carried)
            carried = carried + suf[rows, 0:1] + per_page[rows, 0:1]
        return jnp.concatenate(pieces, axis=1), carried

    bias, cf_new = decay_bias(jnp.concatenate([r[0] for r in lf], axis=0), cf_ref[...])
    cf_ref[...] = cf_new
    s = _dot(qbd, lanes_of(fk)) + bias
    m = m_ref[...]
    m_new = jnp.maximum(m, jnp.max(s, axis=-1, keepdims=True))
    p = jnp.exp(s - m_new)
    alpha = jnp.exp(m - m_new)
    l_ref[...] = alpha * l_ref[...] + jnp.sum(p, axis=-1, keepdims=True)
    acc_ref[...] = alpha * acc_ref[...] + _dot_nt(p.astype(BF), lanes_of(fv))
    m_ref[...] = m_new

    z = _dot(qsbd, lanes_of(sk))
    lsn = _log_sigmoid(-z)
    after, run_new = decay_bias(jnp.concatenate([lsn[:, i * page:(i + 1) * page] for i in range(g)], axis=0),
                                run_ref[...])
    run_ref[...] = run_new
    w = jnp.exp(z + lsn + after)
    accs_ref[...] += _dot_nt(w.astype(BF), lanes_of(sv))

    @pl.when(j == pl.num_programs(1) - 1)
    def _():
        of_ref[0] = _diag_heads(acc_ref[...] / l_ref[...]).astype(of_ref.dtype)
        os_ref[0] = _diag_heads(accs_ref[...]).astype(os_ref.dtype)
        s = _dot(_block_diag_rows(qm_ref[0], MEM_W), mk_ref[0].astype(BF))
        e = jnp.exp(s - jnp.max(s, axis=-1, keepdims=True))
        p = e / jnp.sum(e, axis=-1, keepdims=True)
        om_ref[0] = _diag_heads(_dot_nt(p.astype(BF), mv_ref[0].astype(BF))).astype(om_ref.dtype)


def _decode_attend(page_table, qf, kfn, vfn, lfn, qs, qm, fk, fv, lfc, sk, sv, mk, mv, g):
    nb, npg = page_table.shape
    page = fk.shape[2]
    per_sample = lambda a, b: pl.BlockSpec((1, a, b), lambda i, j, pt: (i, 0, 0))

    def paged(rows, gi):
        return pl.BlockSpec((1, rows, page), lambda i, j, pt: (pt[i * npg + npg - 1 - (j * g + gi)], 0, 0))

    pos = np.arange(page)
    su = jnp.asarray(pos[:, None] > pos[None, :], BF)
    in_specs = [per_sample(1, FOX_W), per_sample(1, FOX_W), per_sample(1, FOX_W), per_sample(8, 1),
                per_sample(1, SB_W), per_sample(1, MEM_W), per_sample(MEM_W, mk.shape[2]),
                per_sample(MEM_W, mk.shape[2]), pl.BlockSpec((page, page), lambda i, j, pt: (0, 0))]
    operands = [qf, kfn, vfn, lfn, qs, qm, mk, mv, su]
    for arr, rows in ((fk, FOX_W), (fv, FOX_W), (lfc, H_FOX), (sk, SB_W), (sv, SB_W)):
        in_specs += [paged(rows, gi) for gi in range(g)]
        operands += [arr] * g
    return pl.pallas_call(
        functools.partial(_decode_body, g=g),
        grid_spec=pltpu.PrefetchScalarGridSpec(
            num_scalar_prefetch=1,
            grid=(nb, npg // g),
            in_specs=in_specs,
            out_specs=[per_sample(1, FOX_W), per_sample(1, SB_W), per_sample(1, MEM_W)],
            scratch_shapes=[pltpu.VMEM((8, 1), F32), pltpu.VMEM((8, 1), F32), pltpu.VMEM((8, FOX_W), F32),
                            pltpu.VMEM((8, 1), F32), pltpu.VMEM((8, 1), F32), pltpu.VMEM((8, SB_W), F32)]),
        out_shape=[jax.ShapeDtypeStruct((nb, 1, w), BF) for w in (FOX_W, SB_W, MEM_W)],
        compiler_params=pltpu.CompilerParams(dimension_semantics=("arbitrary",) * 2,
                                             vmem_limit_bytes=VMEM_LIMIT),
        name="decode_attend",
    )(page_table.reshape(-1), *operands)


def _router_params(w_grp, b_grp, w_exp, b_exp):
    n = N_GROUPS + N_EXPERTS
    wr = jnp.zeros((32, D_MODEL), BF).at[:n].set(jnp.concatenate([w_grp, w_exp], axis=1).T.astype(BF))
    br = jnp.zeros((32, 1), F32).at[:n, 0].set(jnp.concatenate([b_grp, b_exp]))
    return wr, br


def _split_w_in(w_in):
    o1 = 3 * FOX_W
    o2 = o1 + H_FOX
    o3 = o2 + 3 * SB_W + MEM_W
    wa = jnp.concatenate([w_in[:, :o1], w_in[:, o2:o3]], axis=1).astype(BF)
    wf = jnp.zeros((D_MODEL, LANES), BF).at[:, :H_FOX].set(w_in[:, o1:o2].astype(BF))
    return wa, wf, w_in[:, o3:].astype(BF)


def _position_minor(cache):
    pool, page, h, dh = cache.shape
    return jnp.transpose(cache, (0, 2, 3, 1)).reshape(pool, h * dh, page)


def kernel(x_prompt, x_sample, mem_prompt, cache_fox_k, cache_fox_v, cache_fox_logf, cache_sb_k, cache_sb_v,
           cache_mem_k, cache_mem_v, page_table, norm_attn, w_in, b_forget, g_fox_q, g_fox_k, g_mem_q, g_mem_k,
           norm_mem, w_mem_k, w_mem_v, w_up_fox, w_up_sb, w_up_mem, w_o, norm_ffn, w_grp, b_grp, w_exp, b_exp,
           w1, w3, w2):
    assert w_in.shape[0] == 1, "one layer"
    b, s, _ = x_prompt.shape
    nb = x_sample.shape[0]
    wa, wf, wg = _split_w_in(w_in[0])
    wr, br = _router_params(w_grp[0], b_grp[0], w_exp[0], b_exp[0])
    merge_w = (norm_attn[0], wg, w_up_fox[0].astype(BF), w_up_sb[0].astype(BF), w_up_mem[0].astype(BF),
               w_o[0].astype(BF), norm_ffn[0], wr, br)
    moe_w = (w1[0].astype(BF), w3[0].astype(BF), w2[0].astype(BF))
    proj_w = (norm_attn[0], wa, wf, b_forget[0], g_fox_q[0], g_fox_k[0], g_mem_q[0])

    qf, kf, kfb, vf, vfb, lf, cexp, qs, ks, ksb, vs, vsb, qm = _inproj(x_prompt, *proj_w, tm=PROMPT_TM)
    ck = jnp.swapaxes(cexp[..., ::HEAD_DIM], 1, 2).reshape(b, H_FOX, 1, s)
    o_f = _fox_prompt(qf, kfb, vfb, cexp, ck, tq=ATTN_T, tk=FOX_TK)
    o_s = _sb_prompt(qs, ksb, vsb, tq=ATTN_T, tk=SB_TK)
    mk, mkb, mv, mvb = _mem_kv(mem_prompt, norm_mem[0], w_mem_k[0], w_mem_v[0], g_mem_k[0])
    o_m = _mem_attend(qm, mkb, mvb, tm=PROMPT_TM)
    t = b * s
    flat = lambda a: a.reshape(t, a.shape[-1])
    h, hn, eidx, gates = _merge(flat(x_prompt), flat(o_f), flat(o_s), flat(o_m), *merge_w, tm=PROMPT_TM)
    y_prompt = _moe(hn, h, eidx, gates, *moe_w, tt=MOE_TT, r=MOE_R).reshape(b, s, D_MODEL)

    sq = _inproj(x_sample.reshape(1, nb, D_MODEL), *proj_w, tm=nb)
    qf2, kf2, kfb2, vf2, vfb2, lf2, _, qs2, ks2, _, vs2, _, qm2 = sq
    per = lambda a: a.reshape(nb, 1, a.shape[-1])
    o_f2, o_s2, o_m2 = _decode_attend(
        page_table, per(qf2), per(kfb2), per(vfb2), lf2[0, :, :H_FOX].reshape(nb, H_FOX, 1), per(qs2), per(qm2),
        _position_minor(cache_fox_k[0]), _position_minor(cache_fox_v[0]),
        jnp.transpose(cache_fox_logf[0], (0, 2, 1)), _position_minor(cache_sb_k[0]), _position_minor(cache_sb_v[0]),
        _position_minor(cache_mem_k[0]), _position_minor(cache_mem_v[0]), g=DECODE_PAGES_PER_STEP)
    flat2 = lambda a: a.reshape(nb, a.shape[-1])
    h2, hn2, eidx2, gates2 = _merge(flat2(x_sample), flat2(o_f2), flat2(o_s2), flat2(o_m2), *merge_w, tm=nb)
    y_sample = _moe(hn2, h2, eidx2, gates2, *moe_w, tt=nb, r=nb).reshape(nb, 1, D_MODEL)

    heads = lambda a, n: a.reshape(1, a.shape[0], a.shape[1], n, HEAD_DIM)
    dec = lambda a, n: a.reshape(1, nb, 1, n, HEAD_DIM)
    return (y_prompt, y_sample,
            heads(kf, H_FOX), heads(vf, H_FOX), lf[..., :H_FOX].reshape(1, b, s, H_FOX),
            heads(ks, H_SB), heads(vs, H_SB), heads(mk, H_MEM), heads(mv, H_MEM),
            dec(kf2, H_FOX), dec(vf2, H_FOX), lf2[..., :H_FOX].reshape(1, nb, 1, H_FOX),
            dec(ks2, H_SB), dec(vs2, H_SB))
```

```python
import functools

import numpy as np
import jax
import jax.numpy as jnp
from jax import lax
from jax.experimental import pallas as pl
from jax.experimental.pallas import tpu as pltpu

BF = jnp.bfloat16
F32 = jnp.float32

D_MODEL = 1024
HEAD_DIM = 64
H_FOX = 8
H_SB = 4
H_MEM = 4
FOX_W = H_FOX * HEAD_DIM
SB_W = H_SB * HEAD_DIM
MEM_W = H_MEM * HEAD_DIM
N_GROUPS = 4
EXPERTS_PER_GROUP = 4
N_EXPERTS = N_GROUPS * EXPERTS_PER_GROUP
D_EXPERT = 512
EPS = 1e-6
ATTN_SCALE = HEAD_DIM ** -0.5
LANES = 128
A_W = 3 * FOX_W + 3 * SB_W + MEM_W
VMEM_LIMIT = 56 * 1024 * 1024
PROMPT_TM = 256
ATTN_T = 256
MOE_TT = 1024
FOX_TK = 512
SB_TK = 512
MOE_R = 256
DECODE_PAGES_PER_STEP = 16


def _dot(a, b):
    return jnp.dot(a, b, preferred_element_type=F32)


def _dot_nt(a, b):
    return lax.dot_general(a, b, (((1,), (1,)), ((), ())), preferred_element_type=F32)


def _split3(a):
    a1 = a.astype(BF)
    r = a - a1.astype(F32)
    a2 = r.astype(BF)
    a3 = (r - a2.astype(F32)).astype(BF)
    return a1, a2, a3


def _exact_left(m, b):
    b1, b2, b3 = _split3(b)
    return _dot(m, b1) + _dot(m, b2) + _dot(m, b3)


def _exact_right(a, m):
    a1, a2, a3 = _split3(a)
    return _dot(a1, m) + _dot(a2, m) + _dot(a3, m)


def _suffix_sums(a, later):
    hi = a.astype(BF)
    lo = (a - hi.astype(F32)).astype(BF)
    return _dot(hi, later) + _dot(lo, later)


def _log_sigmoid(x):
    return jnp.minimum(x, 0.0) - jnp.log1p(jnp.exp(-jnp.abs(x)))


def _rms(x, g):
    return x * lax.rsqrt(jnp.mean(x * x, axis=-1, keepdims=True) + EPS) * g


def _head_norm(t, head_mean, g):
    ms = _dot((t * t).astype(BF), head_mean)
    return t * lax.rsqrt(ms + EPS) * g


def _const_spec(shape):
    return pl.BlockSpec(shape, lambda *_: (0,) * len(shape))


def _head_mean_matrix(width):
    i = np.arange(width)
    return jnp.asarray((i[:, None] // HEAD_DIM == i[None, :] // HEAD_DIM) / HEAD_DIM, BF)


def _inproj_body(x_ref, na_ref, wa_ref, wf_ref, bfg_ref, gfq_ref, gfk_ref, gmq_ref, hm512_ref, hm256_ref,
                 tri_ref, exp_ref,
                 qf_ref, kf_ref, kfb_ref, vf_ref, vfb_ref, lf_ref, cexp_ref, qs_ref, ks_ref, ksb_ref,
                 vs_ref, vsb_ref, qm_ref, carry_ref):
    tm = x_ref.shape[1]

    @pl.when(pl.program_id(1) == 0)
    def _():
        carry_ref[...] = jnp.zeros_like(carry_ref)

    xn = _rms(x_ref[0], na_ref[...]).astype(BF)
    y = _dot(xn, wa_ref[...])
    f = _dot(xn, wf_ref[...])

    qf = _head_norm(y[:, 0:FOX_W], hm512_ref[...], gfq_ref[...])
    qf_ref[0] = (qf * ATTN_SCALE).astype(BF)
    kf = _head_norm(y[:, FOX_W:2 * FOX_W], hm512_ref[...], gfk_ref[...])
    kf_ref[0] = kf
    kfb_ref[0] = kf.astype(BF)
    vf = y[:, 2 * FOX_W:3 * FOX_W]
    vf_ref[0] = vf
    vfb_ref[0] = vf.astype(BF)
    o = 3 * FOX_W
    qs_ref[0] = (y[:, o:o + SB_W] * ATTN_SCALE).astype(BF)
    ks = y[:, o + SB_W:o + 2 * SB_W]
    ks_ref[0] = ks
    ksb_ref[0] = ks.astype(BF)
    vs = y[:, o + 2 * SB_W:o + 3 * SB_W]
    vs_ref[0] = vs
    vsb_ref[0] = vs.astype(BF)
    qm = _head_norm(y[:, o + 3 * SB_W:o + 3 * SB_W + MEM_W], hm256_ref[...], gmq_ref[...])
    qm_ref[0] = (qm * ATTN_SCALE).astype(BF)

    lane = lax.broadcasted_iota(jnp.int32, (tm, LANES), 1)
    lf = jnp.where(lane < H_FOX, _log_sigmoid(f + bfg_ref[...]), 0.0)
    lf_ref[0] = lf
    c = carry_ref[...] + _exact_left(tri_ref[...], lf)
    carry_ref[...] = c[tm - 1:tm, :]
    cexp_ref[0] = _exact_right(c, exp_ref[...])


def _inproj(x, norm_attn, wa, wf, b_forget, g_fox_q, g_fox_k, g_mem_q, tm):
    b, s, _ = x.shape
    nt = s // tm
    tile = lambda w: pl.BlockSpec((1, tm, w), lambda i, j: (i, j, 0))
    rep = lambda g, n: jnp.tile(g.reshape(1, HEAD_DIM), (1, n))
    bfg = jnp.zeros((1, LANES), F32).at[0, :H_FOX].set(b_forget)
    tri = jnp.asarray(np.tril(np.ones((tm, tm))), BF)
    i = np.arange(FOX_W)
    expand = jnp.asarray(np.arange(LANES)[:, None] == i[None, :] // HEAD_DIM, BF)
    shapes = [(FOX_W, BF), (FOX_W, F32), (FOX_W, BF), (FOX_W, F32), (FOX_W, BF), (LANES, F32), (FOX_W, F32),
              (SB_W, BF), (SB_W, F32), (SB_W, BF), (SB_W, F32), (SB_W, BF), (MEM_W, BF)]
    return pl.pallas_call(
        _inproj_body,
        grid=(b, nt),
        in_specs=[tile(D_MODEL), _const_spec((1, D_MODEL)), _const_spec((D_MODEL, A_W)),
                  _const_spec((D_MODEL, LANES)), _const_spec((1, LANES)), _const_spec((1, FOX_W)),
                  _const_spec((1, FOX_W)), _const_spec((1, MEM_W)), _const_spec((FOX_W, FOX_W)),
                  _const_spec((MEM_W, MEM_W)), _const_spec((tm, tm)), _const_spec((LANES, FOX_W))],
        out_specs=[tile(w) for w, _ in shapes],
        out_shape=[jax.ShapeDtypeStruct((b, s, w), d) for w, d in shapes],
        scratch_shapes=[pltpu.VMEM((1, LANES), F32)],
        compiler_params=pltpu.CompilerParams(dimension_semantics=("arbitrary", "arbitrary"),
                                             vmem_limit_bytes=VMEM_LIMIT),
        name="inproj",
    )(x, norm_attn.reshape(1, D_MODEL), wa, wf, bfg, rep(g_fox_q, H_FOX), rep(g_fox_k, H_FOX),
      rep(g_mem_q, H_MEM), _head_mean_matrix(FOX_W), _head_mean_matrix(MEM_W), tri, expand)


def _head_masked(q, n_heads):
    t = q.shape[0]
    lane = lax.broadcasted_iota(jnp.int32, (t, LANES), 1)
    out = []
    for h in range(n_heads):
        pair = q[:, (h // 2) * LANES:(h // 2 + 1) * LANES]
        out.append(jnp.where(lane // HEAD_DIM == h % 2, pair, 0.0).astype(BF))
    return out


def _merge_pairs(per_head):
    t = per_head[0].shape[0]
    lane = lax.broadcasted_iota(jnp.int32, (t, LANES), 1)
    pairs = [jnp.where(lane < HEAD_DIM, per_head[h], per_head[h + 1]) for h in range(0, len(per_head), 2)]
    return jnp.concatenate(pairs, axis=1)


def _fox_body(q_ref, k_ref, v_ref, cq_ref, ck_ref, o_ref, *, tq, tk):
    qi = pl.program_id(1)
    row = lax.broadcasted_iota(jnp.int32, (tq, tk), 0)
    col = lax.broadcasted_iota(jnp.int32, (tq, tk), 1)
    qh = _head_masked(q_ref[0].astype(F32), H_FOX)
    cq = cq_ref[0]
    cqh = [jnp.broadcast_to(cq[:, h * HEAD_DIM:h * HEAD_DIM + 1], (tq, tk)) for h in range(H_FOX)]

    def step(kt, carry, masked):
        k0 = pl.multiple_of(kt * tk, tk)
        lanes = [pl.ds((h // 2) * LANES, LANES) for h in range(H_FOX)]
        scores = [_dot_nt(qh[h], k_ref[0, pl.ds(k0, tk), lanes[h]]) for h in range(H_FOX)]
        stats = []
        for h in range(H_FOX):
            m, l = carry[3 * h], carry[3 * h + 1]
            s = scores[h] + (cqh[h] - ck_ref[0, h, :, pl.ds(k0, tk)])
            if masked:
                s = jnp.where(k0 + col <= qi * tq + row, s, -jnp.inf)
            m_new = jnp.maximum(m, jnp.max(s, axis=-1, keepdims=True))
            p = jnp.exp(s - m_new)
            alpha = jnp.exp(m - m_new)
            stats.append((m_new, alpha * l + jnp.sum(p, axis=-1, keepdims=True), alpha, p.astype(BF)))
        pv = [_dot(stats[h][3], v_ref[0, pl.ds(k0, tk), lanes[h]]) for h in range(H_FOX)]
        new = []
        for h in range(H_FOX):
            m_new, l, alpha, _ = stats[h]
            new += [m_new, l, alpha * carry[3 * h + 2] + pv[h]]
        return tuple(new)

    init = (jnp.full((tq, 1), -jnp.inf, F32), jnp.zeros((tq, 1), F32), jnp.zeros((tq, LANES), F32)) * H_FOX
    n_full = (qi * tq) // tk
    carry = lax.fori_loop(0, n_full, lambda kt, c: step(kt, c, False), init)
    fin = step(n_full, carry, True)
    o_ref[0] = _merge_pairs([fin[3 * h + 2] / fin[3 * h + 1] for h in range(H_FOX)]).astype(o_ref.dtype)


def _fox_prompt(qf, kfb, vfb, cexp, ck, tq, tk):
    b, s, _ = qf.shape
    assert tk % tq == 0 and s % tk == 0
    qspec = pl.BlockSpec((1, tq, FOX_W), lambda i, j: (i, j, 0))
    kspec = pl.BlockSpec((1, s, FOX_W), lambda i, j: (i, 0, 0))
    return pl.pallas_call(
        functools.partial(_fox_body, tq=tq, tk=tk),
        grid=(b, s // tq),
        in_specs=[qspec, kspec, kspec, qspec, pl.BlockSpec((1, H_FOX, 1, s), lambda i, j: (i, 0, 0, 0))],
        out_specs=qspec,
        out_shape=jax.ShapeDtypeStruct((b, s, FOX_W), BF),
        compiler_params=pltpu.CompilerParams(dimension_semantics=("arbitrary",) * 2,
                                             vmem_limit_bytes=VMEM_LIMIT),
        name="fox_prompt",
    )(qf, kfb, vfb, cexp, ck)


def _sb_body(q_ref, k_ref, v_ref, u_ref, o_ref, *, tq, tk):
    qi = pl.program_id(1)
    row = lax.broadcasted_iota(jnp.int32, (tq, tk), 0)
    col = lax.broadcasted_iota(jnp.int32, (tq, tk), 1)
    qh = _head_masked(q_ref[0].astype(F32), H_SB)
    nblk = tk // LANES

    def split_blocks(lm):
        stacked = jnp.concatenate([lm[:, b * LANES:(b + 1) * LANES] for b in range(nblk)], axis=0)
        hi = stacked.astype(BF)
        return hi, (stacked - hi.astype(F32)).astype(BF)

    def add_later(loc, later):
        out = [None] * nblk
        for b in reversed(range(nblk)):
            lb = loc[b * tq:(b + 1) * tq]
            out[b] = lb + later
            later = later + lb[:, 0:1]
        return jnp.concatenate(out, axis=1), later

    def step(kt, carry, diag):
        k0 = pl.multiple_of(kt * tk, tk)
        valid = k0 + col < qi * tq + row
        heads = range(H_SB)
        lanes = [pl.ds((h // 2) * LANES, LANES) for h in heads]
        zs = [_dot_nt(qh[h], k_ref[0, pl.ds(k0, tk), lanes[h]]) for h in heads]
        parts = []
        for h in heads:
            lsn = jnp.minimum(-zs[h], 0.0) - jnp.log(1.0 + jnp.exp(-jnp.abs(zs[h])))
            parts.append(split_blocks(jnp.where(valid, lsn, 0.0) if diag else lsn))
        locs = [_dot(hi, u_ref[...]) + _dot(lo, u_ref[...]) for hi, lo in parts]
        ws, runs = [], []
        for h in heads:
            incl, run = add_later(locs[h], carry[2 * h])
            w = jnp.exp(zs[h] + incl)
            ws.append((jnp.where(valid, w, 0.0) if diag else w).astype(BF))
            runs.append(run)
        pv = [_dot(ws[h], v_ref[0, pl.ds(k0, tk), lanes[h]]) for h in heads]
        new = []
        for h in heads:
            new += [runs[h], carry[2 * h + 1] + pv[h]]
        return tuple(new)

    init = (jnp.zeros((tq, 1), F32), jnp.zeros((tq, LANES), F32)) * H_SB
    n_full = (qi * tq) // tk
    carry = step(n_full, init, True)
    fin = lax.fori_loop(0, n_full, lambda i, c: step(n_full - 1 - i, c, False), carry)
    o_ref[0] = _merge_pairs([fin[2 * h + 1] for h in range(H_SB)]).astype(o_ref.dtype)


def _sb_prompt(qs, ksb, vsb, tq, tk):
    b, s, _ = qs.shape
    assert tk % tq == 0 and s % tk == 0
    qspec = pl.BlockSpec((1, tq, SB_W), lambda i, j: (i, j, 0))
    kspec = pl.BlockSpec((1, s, SB_W), lambda i, j: (i, 0, 0))
    u = jnp.asarray(np.arange(LANES)[:, None] >= np.arange(LANES)[None, :], BF)
    return pl.pallas_call(
        functools.partial(_sb_body, tq=tq, tk=tk),
        grid=(b, s // tq),
        in_specs=[qspec, kspec, kspec, _const_spec((LANES, LANES))],
        out_specs=qspec,
        out_shape=jax.ShapeDtypeStruct((b, s, SB_W), BF),
        compiler_params=pltpu.CompilerParams(dimension_semantics=("arbitrary",) * 2,
                                             vmem_limit_bytes=VMEM_LIMIT),
        name="sb_prompt",
    )(qs, ksb, vsb, u)


def _memkv_body(mem_ref, nm_ref, wk_ref, wv_ref, gk_ref, hm_ref, mk_ref, mkb_ref, mv_ref, mvb_ref):
    mn = _rms(mem_ref[0], nm_ref[...]).astype(BF)
    mk = _head_norm(_dot(mn, wk_ref[...]), hm_ref[...], gk_ref[...])
    mv = _dot(mn, wv_ref[...])
    mk_ref[0] = mk
    mkb_ref[0] = mk.astype(BF)
    mv_ref[0] = mv
    mvb_ref[0] = mv.astype(BF)


def _mem_kv(mem, norm_mem, w_mem_k, w_mem_v, g_mem_k):
    b, n, _ = mem.shape
    spec = pl.BlockSpec((1, n, MEM_W), lambda i: (i, 0, 0))
    return pl.pallas_call(
        _memkv_body,
        grid=(b,),
        in_specs=[pl.BlockSpec((1, n, D_MODEL), lambda i: (i, 0, 0)), _const_spec((1, D_MODEL)),
                  _const_spec((D_MODEL, MEM_W)), _const_spec((D_MODEL, MEM_W)), _const_spec((1, MEM_W)),
                  _const_spec((MEM_W, MEM_W))],
        out_specs=[spec] * 4,
        out_shape=[jax.ShapeDtypeStruct((b, n, MEM_W), d) for d in (F32, BF, F32, BF)],
        compiler_params=pltpu.CompilerParams(dimension_semantics=("arbitrary",), vmem_limit_bytes=VMEM_LIMIT),
        name="mem_kv",
    )(mem, norm_mem.reshape(1, D_MODEL), w_mem_k.astype(BF), w_mem_v.astype(BF),
      jnp.tile(g_mem_k.reshape(1, HEAD_DIM), (1, H_MEM)), _head_mean_matrix(MEM_W))


def _memattn_body(q_ref, mk_ref, mv_ref, o_ref):
    q = q_ref[0].astype(F32)
    lane = lax.broadcasted_iota(jnp.int32, q.shape, 1)
    out = jnp.zeros(q.shape, F32)
    for h in range(H_MEM):
        qh = jnp.where(lane // HEAD_DIM == h, q, 0.0).astype(BF)
        s = _dot_nt(qh, mk_ref[0])
        e = jnp.exp(s - jnp.max(s, axis=-1, keepdims=True))
        p = e / jnp.sum(e, axis=-1, keepdims=True)
        out = jnp.where(lane // HEAD_DIM == h, _dot(p.astype(BF), mv_ref[0]), out)
    o_ref[0] = out.astype(o_ref.dtype)


def _mem_attend(qm, mkb, mvb, tm):
    b, s, _ = qm.shape
    n = mkb.shape[1]
    qspec = pl.BlockSpec((1, tm, MEM_W), lambda i, j: (i, j, 0))
    kspec = pl.BlockSpec((1, n, MEM_W), lambda i, j: (i, 0, 0))
    return pl.pallas_call(
        _memattn_body,
        grid=(b, s // tm),
        in_specs=[qspec, kspec, kspec],
        out_specs=qspec,
        out_shape=jax.ShapeDtypeStruct((b, s, MEM_W), BF),
        compiler_params=pltpu.CompilerParams(dimension_semantics=("arbitrary",) * 2,
                                             vmem_limit_bytes=VMEM_LIMIT),
        name="mem_attend",
    )(qm, mkb, mvb)


def _merge_body(x_ref, of_ref, os_ref, om_ref, na_ref, wg_ref, wuf_ref, wus_ref, wum_ref, wo_ref, nf_ref,
                wr_ref, br_ref, h_ref, hn_ref, eidx_ref, gate_ref):
    x = x_ref[...]
    tm = x.shape[0]
    xn = _rms(x, na_ref[...]).astype(BF)
    g = jax.nn.sigmoid(_dot(xn, wg_ref[...]))
    m = (g[:, :D_MODEL] * _dot(of_ref[...], wuf_ref[...])
         + g[:, D_MODEL:2 * D_MODEL] * _dot(os_ref[...], wus_ref[...])
         + g[:, 2 * D_MODEL:] * _dot(om_ref[...], wum_ref[...]))
    h = x + _dot(m.astype(BF), wo_ref[...])
    h_ref[...] = h
    hn = _rms(h, nf_ref[...]).astype(BF)
    hn_ref[...] = hn

    lt = _dot_nt(wr_ref[...], hn) + br_ref[...]
    grp = [lt[i:i + 1] for i in range(N_GROUPS)]
    gmax = functools.reduce(jnp.maximum, grp)
    gidx = jnp.where(grp[0] == gmax, 0, jnp.where(grp[1] == gmax, 1, jnp.where(grp[2] == gmax, 2, 3)))
    g_gate = 1.0 / functools.reduce(jnp.add, [jnp.exp(v - gmax) for v in grp])
    ins = []
    for k in range(EXPERTS_PER_GROUP):
        v = jnp.zeros((1, tm), F32)
        for gi in range(N_GROUPS):
            r = N_GROUPS + gi * EXPERTS_PER_GROUP + k
            v = jnp.where(gidx == gi, lt[r:r + 1], v)
        ins.append(v)
    imax = functools.reduce(jnp.maximum, ins)
    ex = [jnp.exp(v - imax) for v in ins]
    tot = functools.reduce(jnp.add, ex)
    p = [v / tot for v in ex]

    def top1(vals):
        best = functools.reduce(jnp.maximum, vals)
        idx = jnp.where(vals[0] == best, 0, jnp.where(vals[1] == best, 1, jnp.where(vals[2] == best, 2, 3)))
        return best, idx

    p1, i1 = top1(p)
    p2, i2 = top1([jnp.where(i1 == k, -1.0, p[k]) for k in range(EXPERTS_PER_GROUP)])
    den = p1 + p2
    row = lax.broadcasted_iota(jnp.int32, (8, tm), 0)
    e0 = gidx * EXPERTS_PER_GROUP + i1
    e1 = gidx * EXPERTS_PER_GROUP + i2
    eidx_ref[...] = jnp.where(row == 0, e0, jnp.where(row == 1, e1, 0))
    gate_ref[...] = jnp.where(row == 0, g_gate * p1 / den, jnp.where(row == 1, g_gate * p2 / den, 0.0))


def _merge(x, o_fox, o_sb, o_mem, norm_attn, wg, w_up_fox, w_up_sb, w_up_mem, w_o, norm_ffn, wr, br, tm):
    t = x.shape[0]
    row = lambda w: pl.BlockSpec((tm, w), lambda i: (i, 0))
    col = pl.BlockSpec((8, tm), lambda i: (0, i))
    return pl.pallas_call(
        _merge_body,
        grid=(t // tm,),
        in_specs=[row(D_MODEL), row(FOX_W), row(SB_W), row(MEM_W), _const_spec((1, D_MODEL)),
                  _const_spec((D_MODEL, 3 * D_MODEL)), _const_spec((FOX_W, D_MODEL)), _const_spec((SB_W, D_MODEL)),
                  _const_spec((MEM_W, D_MODEL)), _const_spec((D_MODEL, D_MODEL)), _const_spec((1, D_MODEL)),
                  _const_spec((32, D_MODEL)), _const_spec((32, 1))],
        out_specs=[row(D_MODEL), row(D_MODEL), col, col],
        out_shape=[jax.ShapeDtypeStruct((t, D_MODEL), F32), jax.ShapeDtypeStruct((t, D_MODEL), BF),
                   jax.ShapeDtypeStruct((8, t), jnp.int32), jax.ShapeDtypeStruct((8, t), F32)],
        compiler_params=pltpu.CompilerParams(dimension_semantics=("arbitrary",), vmem_limit_bytes=VMEM_LIMIT),
        name="merge",
    )(x, o_fox, o_sb, o_mem, norm_attn.reshape(1, D_MODEL), wg, w_up_fox, w_up_sb, w_up_mem, w_o,
      norm_ffn.reshape(1, D_MODEL), wr, br)


def _moe_body(cnt_ref, hn_ref, h_ref, irow_ref, icol_ref, gcol_ref, w1_ref, w3_ref, w2_ref, su_ref, sl_ref,
              y_ref, rrow_ref, rc0_ref, rc1_ref, *, tt, r):
    i = pl.program_id(0)
    e = pl.program_id(1)

    @pl.when(e == 0)
    def _():
        y_ref[...] = h_ref[...]
        erow = lax.broadcasted_iota(jnp.int32, (N_EXPERTS, tt), 0)
        m0 = erow == irow_ref[0:1, :]
        m1 = erow == irow_ref[1:2, :]
        ranks = _dot((m0 | m1).astype(BF), su_ref[...])
        rrow_ref[0:1, :] = jnp.sum(jnp.where(m0, ranks, 0.0), axis=0, keepdims=True)
        rrow_ref[1:2, :] = jnp.sum(jnp.where(m1, ranks, 0.0), axis=0, keepdims=True)
        lane = lax.broadcasted_iota(jnp.int32, (tt, LANES), 1)
        c0 = lane == icol_ref[:, 0:1]
        c1 = lane == icol_ref[:, 1:2]
        rcol = _dot(sl_ref[...], (c0 | c1).astype(BF))
        rc0_ref[...] = jnp.sum(jnp.where(c0, rcol, 0.0), axis=1, keepdims=True)
        rc1_ref[...] = jnp.sum(jnp.where(c1, rcol, 0.0), axis=1, keepdims=True)

    n = cnt_ref[i * N_EXPERTS + e]
    sel_row = jnp.where(irow_ref[0:1, :] == e, rrow_ref[0:1, :],
                        jnp.where(irow_ref[1:2, :] == e, rrow_ref[1:2, :], -1.0))
    a0 = icol_ref[:, 0:1] == e
    a1 = icol_ref[:, 1:2] == e
    sel_col = jnp.where(a0, rc0_ref[...], jnp.where(a1, rc1_ref[...], -1.0))
    gate_col = jnp.where(a0, gcol_ref[:, 0:1], jnp.where(a1, gcol_ref[:, 1:2], 0.0))

    def chunk(c, carry):
        base = (c * r).astype(F32)
        p = (sel_row - base == lax.broadcasted_iota(jnp.int32, (r, tt), 0).astype(F32)).astype(BF)
        pt = (sel_col - base == lax.broadcasted_iota(jnp.int32, (tt, r), 1).astype(F32)).astype(BF)
        xc = _dot(p, hn_ref[...]).astype(BF)
        hid = jax.nn.silu(_dot(xc, w1_ref[0])) * _dot(xc, w3_ref[0])
        out = _dot(hid.astype(BF), w2_ref[0])
        y_ref[...] += gate_col * _dot(pt, out.astype(BF))
        return carry

    lax.fori_loop(0, (n + r - 1) // r, chunk, 0)


def _moe(hn, h, eidx, gates, w1, w3, w2, tt, r):
    t = hn.shape[0]
    nt = t // tt
    ids = eidx[:2]
    onehot = (ids[:, :, None] == jnp.arange(N_EXPERTS)[None, None, :]).any(axis=0)
    cnt = onehot.reshape(nt, tt, N_EXPERTS).sum(axis=1).astype(jnp.int32).reshape(-1)
    icol = jnp.transpose(eidx)
    gcol = jnp.transpose(gates)
    pos = np.arange(tt)
    su = jnp.asarray(pos[:, None] < pos[None, :], BF)
    sl = jnp.asarray(pos[:, None] > pos[None, :], BF)
    tok = lambda w: pl.BlockSpec((tt, w), lambda i, e, c: (i, 0))
    wspec = lambda a, b: pl.BlockSpec((1, a, b), lambda i, e, c: (e, 0, 0))
    return pl.pallas_call(
        functools.partial(_moe_body, tt=tt, r=r),
        grid_spec=pltpu.PrefetchScalarGridSpec(
            num_scalar_prefetch=1,
            grid=(nt, N_EXPERTS),
            in_specs=[tok(D_MODEL), tok(D_MODEL), pl.BlockSpec((8, tt), lambda i, e, c: (0, i)), tok(8), tok(8),
                      wspec(D_MODEL, D_EXPERT), wspec(D_MODEL, D_EXPERT), wspec(D_EXPERT, D_MODEL),
                      pl.BlockSpec((tt, tt), lambda i, e, c: (0, 0)), pl.BlockSpec((tt, tt), lambda i, e, c: (0, 0))],
            out_specs=tok(D_MODEL),
            scratch_shapes=[pltpu.VMEM((8, tt), F32), pltpu.VMEM((tt, 1), F32), pltpu.VMEM((tt, 1), F32)]),
        out_shape=jax.ShapeDtypeStruct((t, D_MODEL), F32),
        compiler_params=pltpu.CompilerParams(dimension_semantics=("arbitrary",) * 2,
                                             vmem_limit_bytes=VMEM_LIMIT),
        name="moe",
    )(cnt, hn, h, eidx, icol, gcol, w1, w3, w2, su, sl)


def _block_diag_rows(q_row, width):
    sub = lax.broadcasted_iota(jnp.int32, (8, width), 0)
    lane = lax.broadcasted_iota(jnp.int32, (8, width), 1)
    return jnp.where(lane // HEAD_DIM == sub, jnp.broadcast_to(q_row.astype(F32), (8, width)), 0.0).astype(BF)


def _diag_heads(acc):
    sub = lax.broadcasted_iota(jnp.int32, acc.shape, 0)
    lane = lax.broadcasted_iota(jnp.int32, acc.shape, 1)
    return jnp.sum(jnp.where(lane // HEAD_DIM == sub, acc, 0.0), axis=0, keepdims=True)


def _decode_body(pt_ref, qf_ref, kfn_ref, vfn_ref, lfn_ref, qs_ref, qm_ref, mk_ref, mv_ref, su_ref, *rest, g):
    fk, fv, lf, sk, sv = (rest[i * g:(i + 1) * g] for i in range(5))
    of_ref, os_ref, om_ref, m_ref, l_ref, acc_ref, cf_ref, run_ref, accs_ref = rest[5 * g:]
    j = pl.program_id(1)
    qbd = _block_diag_rows(qf_ref[0], FOX_W)
    qsbd = _block_diag_rows(qs_ref[0], SB_W)

    @pl.when(j == 0)
    def _():
        m_ref[...] = jnp.sum(qbd.astype(F32) * kfn_ref[0].astype(F32), axis=-1, keepdims=True)
        l_ref[...] = jnp.ones_like(l_ref)
        acc_ref[...] = jnp.broadcast_to(vfn_ref[0].astype(F32), acc_ref.shape)
        cf_ref[...] = lfn_ref[0]
        run_ref[...] = jnp.zeros_like(run_ref)
        accs_ref[...] = jnp.zeros_like(accs_ref)

    page = su_ref.shape[0]
    lanes_of = lambda a: jnp.concatenate([r[0].astype(BF) for r in a], axis=1)

    def decay_bias(per_page, carried):
        suf = _suffix_sums(per_page, su_ref[...])
        pieces = []
        for i in range(g):
            rows = slice(8 * i, 8 * i + 8)
            pieces.append(suf[rows] + carried)
            carried = carried + suf[rows, 0:1] + per_page[rows, 0:1]
        return jnp.concatenate(pieces, axis=1), carried

    bias, cf_new = decay_bias(jnp.concatenate([r[0] for r in lf], axis=0), cf_ref[...])
    cf_ref[...] = cf_new
    s = _dot(qbd, lanes_of(fk)) + bias
    m = m_ref[...]
    m_new = jnp.maximum(m, jnp.max(s, axis=-1, keepdims=True))
    p = jnp.exp(s - m_new)
    alpha = jnp.exp(m - m_new)
    l_ref[...] = alpha * l_ref[...] + jnp.sum(p, axis=-1, keepdims=True)
    acc_ref[...] = alpha * acc_ref[...] + _dot_nt(p.astype(BF), lanes_of(fv))
    m_ref[...] = m_new

    z = _dot(qsbd, lanes_of(sk))
    lsn = _log_sigmoid(-z)
    after, run_new = decay_bias(jnp.concatenate([lsn[:, i * page:(i + 1) * page] for i in range(g)], axis=0),
                                run_ref[...])
    run_ref[...] = run_new
    w = jnp.exp(z + lsn + after)
    accs_ref[...] += _dot_nt(w.astype(BF), lanes_of(sv))

    @pl.when(j == pl.num_programs(1) - 1)
    def _():
        of_ref[0] = _diag_heads(acc_ref[...] / l_ref[...]).astype(of_ref.dtype)
        os_ref[0] = _diag_heads(accs_ref[...]).astype(os_ref.dtype)
        s = _dot(_block_diag_rows(qm_ref[0], MEM_W), mk_ref[0].astype(BF))
        e = jnp.exp(s - jnp.max(s, axis=-1, keepdims=True))
        p = e / jnp.sum(e, axis=-1, keepdims=True)
        om_ref[0] = _diag_heads(_dot_nt(p.astype(BF), mv_ref[0].astype(BF))).astype(om_ref.dtype)


def _decode_attend(page_table, qf, kfn, vfn, lfn, qs, qm, fk, fv, lfc, sk, sv, mk, mv, g):
    nb, npg = page_table.shape
    page = fk.shape[2]
    per_sample = lambda a, b: pl.BlockSpec((1, a, b), lambda i, j, pt: (i, 0, 0))

    def paged(rows, gi):
        return pl.BlockSpec((1, rows, page), lambda i, j, pt: (pt[i * npg + npg - 1 - (j * g + gi)], 0, 0))

    pos = np.arange(page)
    su = jnp.asarray(pos[:, None] > pos[None, :], BF)
    in_specs = [per_sample(1, FOX_W), per_sample(1, FOX_W), per_sample(1, FOX_W), per_sample(8, 1),
                per_sample(1, SB_W), per_sample(1, MEM_W), per_sample(MEM_W, mk.shape[2]),
                per_sample(MEM_W, mk.shape[2]), pl.BlockSpec((page, page), lambda i, j, pt: (0, 0))]
    operands = [qf, kfn, vfn, lfn, qs, qm, mk, mv, su]
    for arr, rows in ((fk, FOX_W), (fv, FOX_W), (lfc, H_FOX), (sk, SB_W), (sv, SB_W)):
        in_specs += [paged(rows, gi) for gi in range(g)]
        operands += [arr] * g
    return pl.pallas_call(
        functools.partial(_decode_body, g=g),
        grid_spec=pltpu.PrefetchScalarGridSpec(
            num_scalar_prefetch=1,
            grid=(nb, npg // g),
            in_specs=in_specs,
            out_specs=[per_sample(1, FOX_W), per_sample(1, SB_W), per_sample(1, MEM_W)],
            scratch_shapes=[pltpu.VMEM((8, 1), F32), pltpu.VMEM((8, 1), F32), pltpu.VMEM((8, FOX_W), F32),
                            pltpu.VMEM((8, 1), F32), pltpu.VMEM((8, 1), F32), pltpu.VMEM((8, SB_W), F32)]),
        out_shape=[jax.ShapeDtypeStruct((nb, 1, w), BF) for w in (FOX_W, SB_W, MEM_W)],
        compiler_params=pltpu.CompilerParams(dimension_semantics=("arbitrary",) * 2,
                                             vmem_limit_bytes=VMEM_LIMIT),
        name="decode_attend",
    )(page_table.reshape(-1), *operands)


def _router_params(w_grp, b_grp, w_exp, b_exp):
    n = N_GROUPS + N_EXPERTS
    wr = jnp.zeros((32, D_MODEL), BF).at[:n].set(jnp.concatenate([w_grp, w_exp], axis=1).T.astype(BF))
    br = jnp.zeros((32, 1), F32).at[:n, 0].set(jnp.concatenate([b_grp, b_exp]))
    return wr, br


def _split_w_in(w_in):
    o1 = 3 * FOX_W
    o2 = o1 + H_FOX
    o3 = o2 + 3 * SB_W + MEM_W
    wa = jnp.concatenate([w_in[:, :o1], w_in[:, o2:o3]], axis=1).astype(BF)
    wf = jnp.zeros((D_MODEL, LANES), BF).at[:, :H_FOX].set(w_in[:, o1:o2].astype(BF))
    return wa, wf, w_in[:, o3:].astype(BF)


def _position_minor(cache):
    pool, page, h, dh = cache.shape
    return jnp.transpose(cache, (0, 2, 3, 1)).reshape(pool, h * dh, page)


def kernel(x_prompt, x_sample, mem_prompt, cache_fox_k, cache_fox_v, cache_fox_logf, cache_sb_k, cache_sb_v,
           cache_mem_k, cache_mem_v, page_table, norm_attn, w_in, b_forget, g_fox_q, g_fox_k, g_mem_q, g_mem_k,
           norm_mem, w_mem_k, w_mem_v, w_up_fox, w_up_sb, w_up_mem, w_o, norm_ffn, w_grp, b_grp, w_exp, b_exp,
           w1, w3, w2):
    assert w_in.shape[0] == 1, "one layer"
    b, s, _ = x_prompt.shape
    nb = x_sample.shape[0]
    wa, wf, wg = _split_w_in(w_in[0])
    wr, br = _router_params(w_grp[0], b_grp[0], w_exp[0], b_exp[0])
    merge_w = (norm_attn[0], wg, w_up_fox[0].astype(BF), w_up_sb[0].astype(BF), w_up_mem[0].astype(BF),
               w_o[0].astype(BF), norm_ffn[0], wr, br)
    moe_w = (w1[0].astype(BF), w3[0].astype(BF), w2[0].astype(BF))
    proj_w = (norm_attn[0], wa, wf, b_forget[0], g_fox_q[0], g_fox_k[0], g_mem_q[0])

    qf, kf, kfb, vf, vfb, lf, cexp, qs, ks, ksb, vs, vsb, qm = _inproj(x_prompt, *proj_w, tm=PROMPT_TM)
    ck = jnp.swapaxes(cexp[..., ::HEAD_DIM], 1, 2).reshape(b, H_FOX, 1, s)
    o_f = _fox_prompt(qf, kfb, vfb, cexp, ck, tq=ATTN_T, tk=FOX_TK)
    o_s = _sb_prompt(qs, ksb, vsb, tq=ATTN_T, tk=SB_TK)
    mk, mkb, mv, mvb = _mem_kv(mem_prompt, norm_mem[0], w_mem_k[0], w_mem_v[0], g_mem_k[0])
    o_m = _mem_attend(qm, mkb, mvb, tm=PROMPT_TM)
    t = b * s
    flat = lambda a: a.reshape(t, a.shape[-1])
    h, hn, eidx, gates = _merge(flat(x_prompt), flat(o_f), flat(o_s), flat(o_m), *merge_w, tm=PROMPT_TM)
    y_prompt = _moe(hn, h, eidx, gates, *moe_w, tt=MOE_TT, r=MOE_R).reshape(b, s, D_MODEL)

    sq = _inproj(x_sample.reshape(1, nb, D_MODEL), *proj_w, tm=nb)
    qf2, kf2, kfb2, vf2, vfb2, lf2, _, qs2, ks2, _, vs2, _, qm2 = sq
    per = lambda a: a.reshape(nb, 1, a.shape[-1])
    o_f2, o_s2, o_m2 = _decode_attend(
        page_table, per(qf2), per(kfb2), per(vfb2), lf2[0, :, :H_FOX].reshape(nb, H_FOX, 1), per(qs2), per(qm2),
        _position_minor(cache_fox_k[0]), _position_minor(cache_fox_v[0]),
        jnp.transpose(cache_fox_logf[0], (0, 2, 1)), _position_minor(cache_sb_k[0]), _position_minor(cache_sb_v[0]),
        _position_minor(cache_mem_k[0]), _position_minor(cache_mem_v[0]), g=DECODE_PAGES_PER_STEP)
    flat2 = lambda a: a.reshape(nb, a.shape[-1])
    h2, hn2, eidx2, gates2 = _merge(flat2(x_sample), flat2(o_f2), flat2(o_s2), flat2(o_m2), *merge_w, tm=nb)
    y_sample = _moe(hn2, h2, eidx2, gates2, *moe_w, tt=nb, r=nb).reshape(nb, 1, D_MODEL)

    heads = lambda a, n: a.reshape(1, a.shape[0], a.shape[1], n, HEAD_DIM)
    dec = lambda a, n: a.reshape(1, nb, 1, n, HEAD_DIM)
    return (y_prompt, y_sample,
            heads(kf, H_FOX), heads(vf, H_FOX), lf[..., :H_FOX].reshape(1, b, s, H_FOX),
            heads(ks, H_SB), heads(vs, H_SB), heads(mk, H_MEM), heads(mv, H_MEM),
            dec(kf2, H_FOX), dec(vf2, H_FOX), lf2[..., :H_FOX].reshape(1, nb, 1, H_FOX),
            dec(ks2, H_SB), dec(vs2, H_SB))
```
